```python
import math
import jax, jax.numpy as jnp
from jax import lax
import numpy as np

D_MODEL = 1024
BATCH = 16
SEQ = 4096
DEPTH = 1
DEC_BATCH = 32
DEC_SEQ = 64
PAST_LEN = 4096

CHUNK = 64
Q_BLOCK = 128
EPS = 1e-6
SSM_HEADS = 16
SSM_HEAD_DIM = 64
SSM_INNER = SSM_HEADS * SSM_HEAD_DIM
SSM_GROUPS = 2
SSM_STATE = 128
CONV_W = 4
CONV_DIM = SSM_INNER + 2 * SSM_GROUPS * SSM_STATE
SB_HEADS = 8
SB_HEAD_DIM = 64
SB_INNER = SB_HEADS * SB_HEAD_DIM
MEM_LEN = 256
MEM_HEADS = 4
MEM_HEAD_DIM = D_MODEL // MEM_HEADS
MOE_GROUPS = 4
EXPERTS_PER_GROUP = 4
N_EXPERTS = MOE_GROUPS * EXPERTS_PER_GROUP
TOP_K_INNER = 2
D_FF_EXPERT = 512
IN_SIZES = (SSM_INNER, CONV_DIM, SSM_HEADS, SB_INNER, SB_INNER, SB_INNER, D_MODEL, D_MODEL)
IN_COLS = sum(IN_SIZES)

kernel_name = "hybrid_ssd_stickbreak_hmoe_stream_step"


def rmsnorm(x, g):
    xf = x.astype(jnp.float32)
    r = lax.rsqrt(jnp.mean(xf * xf, axis=-1, keepdims=True) + EPS)
    return (xf * r).astype(x.dtype) * g


def split_cols(proj):
    idx = np.cumsum(np.array(IN_SIZES))[:-1].tolist()
    return jnp.split(proj, idx, axis=-1)


def ssd_chunked(xh, dt, a, bm, cm, h0):
    f32 = jnp.float32
    bsz, L, H, P = xh.shape
    c = min(CHUNK, L)
    nc = L // c
    rep = H // SSM_GROUPS
    xf = xh.astype(f32).reshape(bsz, nc, c, H, P)
    bf = jnp.repeat(bm.astype(f32), rep, axis=2).reshape(bsz, nc, c, H, SSM_STATE)
    cf = jnp.repeat(cm.astype(f32), rep, axis=2).reshape(bsz, nc, c, H, SSM_STATE)
    dtc = dt.astype(f32).reshape(bsz, nc, c, H)
    cum = jnp.cumsum(dtc * a, axis=2)
    tri = jnp.tril(jnp.ones((c, c), dtype=bool))
    seg = cum[:, :, :, None, :] - cum[:, :, None, :, :]
    decay = jnp.exp(jnp.where(tri[None, None, :, :, None], seg, -jnp.inf))
    xdt = xf * dtc[..., None]
    cb = jnp.einsum('bzthn,bzshn->bztsh', cf, bf)
    y_intra = jnp.einsum('bztsh,bzshp->bzthp', cb * decay, xdt)
    to_end = jnp.exp(cum[:, :, -1:, :] - cum)
    states = jnp.einsum('bzshn,bzshp->bzhpn', bf, xdt * to_end[..., None])
    chunk_decay = jnp.exp(cum[:, :, -1, :])

    def carry_step(hc, inp):
        s_z, d_z = inp
        return hc * d_z[:, :, None, None] + s_z, hc

    h_last, h_in = lax.scan(carry_step, h0.astype(f32),
                            (jnp.moveaxis(states, 1, 0), jnp.moveaxis(chunk_decay, 1, 0)))
    h_in = jnp.moveaxis(h_in, 0, 1)
    y_inter = jnp.einsum('bzthn,bzhpn->bzthp', cf * jnp.exp(cum)[..., None], h_in)
    y = (y_intra + y_inter).reshape(bsz, L, H, P)
    return y.astype(xh.dtype), h_last.astype(h0.dtype)


def stick_breaking(q, k_all, v_all, q_offset):
    f32 = jnp.float32
    bsz, L, H, d = q.shape
    blk = min(Q_BLOCK, L)
    nb = L // blk
    k_pos = jnp.arange(k_all.shape[1], dtype=jnp.int32)
    scale = SB_HEAD_DIM ** -0.5

    def one_block(args):
        qb, i = args
        q_pos = q_offset + i * blk + jnp.arange(blk, dtype=jnp.int32)
        z = jnp.einsum('bqhd,bkhd->bhqk', qb, k_all).astype(f32) * scale
        causal = k_pos[None, :] < q_pos[:, None]
        log_keep = jnp.where(causal, jax.nn.log_sigmoid(-z), 0.0)
        after = lax.cumsum(log_keep, axis=3, reverse=True) - log_keep
        w = jnp.where(causal, jnp.exp(jax.nn.log_sigmoid(z) + after), 0.0)
        return jnp.einsum('bhqk,bkhd->bqhd', w.astype(v_all.dtype), v_all)

    qb = jnp.moveaxis(q.reshape(bsz, nb, blk, H, d), 1, 0)
    o = lax.map(one_block, (qb, jnp.arange(nb, dtype=jnp.int32)))
    return jnp.moveaxis(o, 0, 1).reshape(bsz, L, H, d)


def mixer_block(x, conv_ctx, ssm_h0, past_k, past_v, norm_mix, w_in, conv_w, conv_b, dt_bias,
                a_log, d_skip, ssm_norm, w_ssm_br, w_sb_br, w_out):
    bsz, L, _ = x.shape
    h = rmsnorm(x, norm_mix)
    z, xbc_raw, dt_raw, q, k, v, gl_ssm, gl_sb = split_cols(h @ w_in)
    xbc_all = jnp.concatenate([conv_ctx, xbc_raw], axis=1)
    new_conv = xbc_all[:, -(CONV_W - 1):]
    conv = conv_b
    for j in range(CONV_W):
        conv = conv + xbc_all[:, j:j + L] * conv_w[j]
    xbc = jax.nn.silu(conv)
    xs, bm, cm = jnp.split(xbc, [SSM_INNER, SSM_INNER + SSM_GROUPS * SSM_STATE], axis=-1)
    xh = xs.reshape(bsz, L, SSM_HEADS, SSM_HEAD_DIM)
    bm = bm.reshape(bsz, L, SSM_GROUPS, SSM_STATE)
    cm = cm.reshape(bsz, L, SSM_GROUPS, SSM_STATE)
    dt = jax.nn.softplus((dt_raw + dt_bias).astype(jnp.float32))
    a = -jnp.exp(a_log.astype(jnp.float32))
    y, h_last = ssd_chunked(xh, dt, a, bm, cm, ssm_h0)
    y = (y + d_skip[:, None] * xh).reshape(bsz, L, SSM_INNER)
    y_ssm = rmsnorm(y * jax.nn.silu(z), ssm_norm) @ w_ssm_br
    qh = q.reshape(bsz, L, SB_HEADS, SB_HEAD_DIM)
    kh = k.reshape(bsz, L, SB_HEADS, SB_HEAD_DIM)
    vh = v.reshape(bsz, L, SB_HEADS, SB_HEAD_DIM)
    if past_k is None:
        k_all, v_all, off = kh, vh, 0
    else:
        k_all = jnp.concatenate([past_k, kh], axis=1)
        v_all = jnp.concatenate([past_v, vh], axis=1)
        off = past_k.shape[1]
    y_sb = stick_breaking(qh, k_all, v_all, off).reshape(bsz, L, SB_INNER) @ w_sb_br
    merged = jax.nn.sigmoid(gl_ssm) * y_ssm + jax.nn.sigmoid(gl_sb) * y_sb
    return x + merged @ w_out, new_conv, h_last, kh, vh


def memory_kv(mem, norm_mem_kv, w_mem_kv):
    bsz, M, _ = mem.shape
    k, v = jnp.split(rmsnorm(mem, norm_mem_kv) @ w_mem_kv, 2, axis=-1)
    return (k.reshape(bsz, M, MEM_HEADS, MEM_HEAD_DIM), v.reshape(bsz, M, MEM_HEADS, MEM_HEAD_DIM))


def memory_block(x, mk, mv, norm_mem_q, w_mem_q, w_mem_o):
    bsz, L, _ = x.shape
    q = (rmsnorm(x, norm_mem_q) @ w_mem_q).reshape(bsz, L, MEM_HEADS, MEM_HEAD_DIM)
    s = jnp.einsum('bqhd,bkhd->bhqk', q, mk).astype(jnp.float32) * (MEM_HEAD_DIM ** -0.5)
    p = jax.nn.softmax(s, axis=-1).astype(mv.dtype)
    o = jnp.einsum('bhqk,bkhd->bqhd', p, mv).reshape(bsz, L, D_MODEL)
    return x + o @ w_mem_o


def hier_moe(x, norm_ffn, w_group_router, w_expert_router, w_gate_e, w_up_e, w_down_e):
    bsz, L, _ = x.shape
    h = rmsnorm(x, norm_ffn).reshape(bsz * L, D_MODEL)
    p_group = jax.nn.softmax((h @ w_group_router).astype(jnp.float32), axis=-1)
    p_sel, g_sel = lax.top_k(p_group, 1)
    le = (h @ w_expert_router).astype(jnp.float32).reshape(-1, MOE_GROUPS, EXPERTS_PER_GROUP)
    le_sel = jnp.take_along_axis(le, g_sel[:, :, None], axis=1)[:, 0]
    top_v, top_i = lax.top_k(le_sel, TOP_K_INNER)
    gate = jax.nn.softmax(top_v, axis=-1) * p_sel
    eid = g_sel * EXPERTS_PER_GROUP + top_i
    dense_w = jnp.sum(jax.nn.one_hot(eid, N_EXPERTS, dtype=jnp.float32) * gate[..., None], axis=1)
    dense_w = dense_w.astype(h.dtype)
    out = jnp.zeros_like(h)
    for e in range(N_EXPERTS):
        he = jax.nn.silu(h @ w_gate_e[e]) * (h @ w_up_e[e])
        out = out + dense_w[:, e:e + 1] * (he @ w_down_e[e])
    return x + out.reshape(bsz, L, D_MODEL)


def setup_inputs(seed: int = 0) -> dict:
    key = jax.random.key(seed)
    ks = jax.random.split(key, 40)
    f32 = jnp.float32
    nrm = lambda k, shape, s: jax.random.normal(k, shape, f32) * s
    gain = lambda k, shape: 1.0 + 0.01 * jax.random.normal(k, shape, f32)
    dt0 = jnp.exp(jax.random.uniform(ks[20], (DEPTH, SSM_HEADS), f32) * (math.log(0.1) - math.log(0.001)) + math.log(0.001))
    return {
        "x_prompt": nrm(ks[0], (BATCH, SEQ, D_MODEL), 1.0),
        "x_sample": nrm(ks[1], (DEC_BATCH, DEC_SEQ, D_MODEL), 1.0),
        "cache_sb_k": nrm(ks[2], (DEPTH, DEC_BATCH, PAST_LEN, SB_HEADS, SB_HEAD_DIM), 1.0),
        "cache_sb_v": nrm(ks[3], (DEPTH, DEC_BATCH, PAST_LEN, SB_HEADS, SB_HEAD_DIM), 1.0),
        "state_ssm": nrm(ks[4], (DEPTH, DEC_BATCH, SSM_HEADS, SSM_HEAD_DIM, SSM_STATE), 0.5),
        "state_conv": nrm(ks[5], (DEPTH, DEC_BATCH, CONV_W - 1, CONV_DIM), 1.0),
        "cache_mem_k": nrm(ks[6], (DEPTH, DEC_BATCH, MEM_LEN, MEM_HEADS, MEM_HEAD_DIM), 1.0),
        "cache_mem_v": nrm(ks[7], (DEPTH, DEC_BATCH, MEM_LEN, MEM_HEADS, MEM_HEAD_DIM), 1.0),
        "mem_prompt": nrm(ks[8], (BATCH, MEM_LEN, D_MODEL), 1.0),
        "norm_mix": gain(ks[9], (DEPTH, D_MODEL)),
        "w_in": nrm(ks[10], (DEPTH, D_MODEL, IN_COLS), D_MODEL ** -0.5),
        "conv_w": nrm(ks[11], (DEPTH, CONV_W, CONV_DIM), CONV_W ** -0.5),
        "conv_b": nrm(ks[12], (DEPTH, CONV_DIM), 0.01),
        "dt_bias": dt0 + jnp.log(-jnp.expm1(-dt0)),
        "a_log": jnp.log(jax.random.uniform(ks[13], (DEPTH, SSM_HEADS), f32, 1.0, 16.0)),
        "d_skip": gain(ks[14], (DEPTH, SSM_HEADS)),
        "ssm_norm": gain(ks[15], (DEPTH, SSM_INNER)),
        "w_ssm_br": nrm(ks[16], (DEPTH, SSM_INNER, D_MODEL), SSM_INNER ** -0.5),
        "w_sb_br": nrm(ks[17], (DEPTH, SB_INNER, D_MODEL), SB_INNER ** -0.5),
        "w_out": nrm(ks[18], (DEPTH, D_MODEL, D_MODEL), D_MODEL ** -0.5),
        "norm_mem_q": gain(ks[19], (DEPTH, D_MODEL)),
        "norm_mem_kv": gain(ks[21], (DEPTH, D_MODEL)),
        "w_mem_q": nrm(ks[22], (DEPTH, D_MODEL, D_MODEL), D_MODEL ** -0.5),
        "w_mem_kv": nrm(ks[23], (DEPTH, D_MODEL, 2 * D_MODEL), D_MODEL ** -0.5),
        "w_mem_o": nrm(ks[24], (DEPTH, D_MODEL, D_MODEL), D_MODEL ** -0.5),
        "norm_ffn": gain(ks[25], (DEPTH, D_MODEL)),
        "w_group_router": nrm(ks[26], (DEPTH, D_MODEL, MOE_GROUPS), D_MODEL ** -0.5),
        "w_expert_router": nrm(ks[27], (DEPTH, D_MODEL, N_EXPERTS), D_MODEL ** -0.5),
        "w_gate_e": nrm(ks[28], (DEPTH, N_EXPERTS, D_MODEL, D_FF_EXPERT), D_MODEL ** -0.5),
        "w_up_e": nrm(ks[29], (DEPTH, N_EXPERTS, D_MODEL, D_FF_EXPERT), D_MODEL ** -0.5),
        "w_down_e": nrm(ks[30], (DEPTH, N_EXPERTS, D_FF_EXPERT, D_MODEL), D_FF_EXPERT ** -0.5),
        "norm_final": gain(ks[31], (D_MODEL,)),
    }


def reference(x_prompt, x_sample, cache_sb_k, cache_sb_v, state_ssm, state_conv, cache_mem_k,
              cache_mem_v, mem_prompt, norm_mix, w_in, conv_w, conv_b, dt_bias, a_log, d_skip,
              ssm_norm, w_ssm_br, w_sb_br, w_out, norm_mem_q, norm_mem_kv, w_mem_q, w_mem_kv,
              w_mem_o, norm_ffn, w_group_router, w_expert_router, w_gate_e, w_up_e, w_down_e,
              norm_final):
    xp, xs = x_prompt, x_sample
    bp = xp.shape[0]
    skp, svp, ssp, scp, mkp, mvp = [], [], [], [], [], []
    sks, svs, sss, scs = [], [], [], []
    for l in range(DEPTH):
        mw = (norm_mix[l], w_in[l], conv_w[l], conv_b[l], dt_bias[l], a_log[l], d_skip[l],
              ssm_norm[l], w_ssm_br[l], w_sb_br[l], w_out[l])
        mq = (norm_mem_q[l], w_mem_q[l], w_mem_o[l])
        fw = (norm_ffn[l], w_group_router[l], w_expert_router[l], w_gate_e[l], w_up_e[l], w_down_e[l])
        conv0 = jnp.zeros((bp, CONV_W - 1, CONV_DIM), xp.dtype)
        h0 = jnp.zeros((bp, SSM_HEADS, SSM_HEAD_DIM, SSM_STATE), xp.dtype)
        xp, conv_p, ssm_p, k_p, v_p = mixer_block(xp, conv0, h0, None, None, *mw)
        mk_p, mv_p = memory_kv(mem_prompt, norm_mem_kv[l], w_mem_kv[l])
        xp = memory_block(xp, mk_p, mv_p, *mq)
        xp = hier_moe(xp, *fw)
        skp.append(k_p); svp.append(v_p); ssp.append(ssm_p); scp.append(conv_p)
        mkp.append(mk_p); mvp.append(mv_p)
        xs, conv_s, ssm_s, k_s, v_s = mixer_block(xs, state_conv[l], state_ssm[l], cache_sb_k[l],
                                                  cache_sb_v[l], *mw)
        xs = memory_block(xs, cache_mem_k[l], cache_mem_v[l], *mq)
        xs = hier_moe(xs, *fw)
        sks.append(k_s); svs.append(v_s); sss.append(ssm_s); scs.append(conv_s)
    y_prompt = rmsnorm(xp, norm_final)
    y_sample = rmsnorm(xs, norm_final)
    return (y_prompt, y_sample, jnp.stack(skp), jnp.stack(svp), jnp.stack(ssp), jnp.stack(scp),
            jnp.stack(mkp), jnp.stack(mvp), jnp.stack(sks), jnp.stack(svs), jnp.stack(sss),
            jnp.stack(scs))
```

```python
import functools

import jax
import jax.numpy as jnp
from jax import lax
from jax.experimental import pallas as pl
from jax.experimental.pallas import tpu as pltpu

F32 = jnp.float32
BF16 = jnp.bfloat16
EPS = 1e-6

D_MODEL = 1024
SSM_HEADS = 16
SSM_HEAD_DIM = 64
SSM_INNER = SSM_HEADS * SSM_HEAD_DIM
SSM_GROUPS = 2
SSM_STATE = 128
CONV_W = 4
CONV_DIM = SSM_INNER + 2 * SSM_GROUPS * SSM_STATE
SB_HEADS = 8
SB_HEAD_DIM = 64
SB_INNER = SB_HEADS * SB_HEAD_DIM
MEM_HEADS = 4
MEM_HEAD_DIM = D_MODEL // MEM_HEADS
MOE_GROUPS = 4
EXPERTS_PER_GROUP = 4
N_EXPERTS = MOE_GROUPS * EXPERTS_PER_GROUP
IN_SIZES = (SSM_INNER, CONV_DIM, SSM_HEADS, SB_INNER, SB_INNER, SB_INNER, D_MODEL, D_MODEL)

LANES_V7X = 128
SUBLANES_V7X = 8
VMEM_BYTES_V7X = 64 * 1024 * 1024
COMPILER_TEMP_BYTES = 20 * 1024 * 1024
VMEM_CAP_BYTES = 60000 * 1024

NT_DIMS = (((1,), (1,)), ((), ()))


def _vmem_limit(block_bytes, scratch_bytes=0):
    need = 2 * block_bytes + scratch_bytes + COMPILER_TEMP_BYTES
    return int(min(need, VMEM_CAP_BYTES))


def _nbytes(shape, dtype):
    n = 1
    for s in shape:
        n *= s
    return n * jnp.dtype(dtype).itemsize


def _sigmoid(x):
    return 1.0 / (1.0 + jnp.exp(-x))


def _softplus(x):
    return jnp.maximum(x, 0.0) + jnp.log1p(jnp.exp(-jnp.abs(x)))


def _rms_scale(x):
    return lax.rsqrt(jnp.mean(x * x, axis=-1, keepdims=True) + EPS)


def _norm_proj_kernel(x_ref, g_ref, *refs, out_plan):
    n_w = len(out_plan)
    w_refs = refs[:n_w]
    o_refs = refs[n_w:]
    x = x_ref[...]
    h = ((x * _rms_scale(x)) * g_ref[...]).astype(BF16)
    oi = 0
    for w_ref, dtypes in zip(w_refs, out_plan):
        y = jnp.dot(h, w_ref[...], preferred_element_type=F32)
        for dt in dtypes:
            o_refs[oi][...] = y.astype(dt)
            oi += 1


def _norm_proj(x2d, gain, weights, out_plan, tm, name):
    t, d = x2d.shape
    assert t % tm == 0
    in_specs = [pl.BlockSpec((tm, d), lambda i: (i, 0)), pl.BlockSpec((1, d), lambda i: (0, 0))]
    block_bytes = _nbytes((tm, d), F32) + _nbytes((1, d), F32)
    for w in weights:
        in_specs.append(pl.BlockSpec(w.shape, lambda i: (0, 0)))
        block_bytes += _nbytes(w.shape, w.dtype)
    out_shapes, out_specs = [], []
    for w, dtypes in zip(weights, out_plan):
        n = w.shape[1]
        for dt in dtypes:
            out_shapes.append(jax.ShapeDtypeStruct((t, n), dt))
            out_specs.append(pl.BlockSpec((tm, n), lambda i: (i, 0)))
            block_bytes += _nbytes((tm, max(n, LANES_V7X)), dt)
    return pl.pallas_call(
        functools.partial(_norm_proj_kernel, out_plan=tuple(out_plan)),
        grid=(t // tm,),
        in_specs=in_specs,
        out_specs=out_specs,
        out_shape=out_shapes,
        compiler_params=pltpu.CompilerParams(
            dimension_semantics=("parallel",), vmem_limit_bytes=_vmem_limit(block_bytes)),
        name=name,
    )(x2d, gain.reshape(1, d), *weights)


def _ssd_kernel(xbc_ref, dt_ref, z_ref, ctx_ref, h0_ref, convw_ref, convb_ref, dtb_ref, alog_ref,
                dskip_ref, norm_ref, yn_ref, ht_ref, ext_ref, y_ref, *, c):
    j = pl.program_id(1)
    halo = SUBLANES_V7X
    n = SSM_STATE
    pair_w = 2 * SSM_HEAD_DIM

    @pl.when(j == 0)
    def _():
        ext_ref[0:halo, :] = ctx_ref[0]
        ht_ref[0] = h0_ref[0]

    raw = xbc_ref[0]
    ext_ref[halo:halo + c, :] = raw
    conv = convb_ref[...]
    for tap in range(CONV_W - 1):
        off = halo - (CONV_W - 1) + tap
        conv = conv + ext_ref[off:off + c, :] * convw_ref[tap:tap + 1, :]
    conv = conv + raw * convw_ref[CONV_W - 1:CONV_W, :]
    ext_ref[0:halo, :] = ext_ref[c:c + halo, :]
    act = conv * _sigmoid(conv)

    dt = _softplus(dt_ref[0] + dtb_ref[...])
    a = -jnp.exp(alog_ref[...])
    row = lax.broadcasted_iota(jnp.int32, (c, c), 0)
    col = lax.broadcasted_iota(jnp.int32, (c, c), 1)
    tril = row >= col
    cum = jnp.dot(tril.astype(F32), dt * a, precision=lax.Precision.HIGHEST,
                  preferred_element_type=F32)
    hrow = lax.broadcasted_iota(jnp.int32, (SSM_HEADS, SSM_HEADS), 0)
    hcol = lax.broadcasted_iota(jnp.int32, (SSM_HEADS, SSM_HEADS), 1)
    eye_h = (hrow == hcol).astype(F32)
    cum_t = lax.dot_general(eye_h, cum, NT_DIMS, precision=lax.Precision.HIGHEST,
                            preferred_element_type=F32)
    cum_last = cum[c - 1:c, :]
    exp_cum = jnp.exp(cum)
    dt_end = dt * jnp.exp(cum_last - cum)
    chunk_decay = jnp.exp(cum_last)

    nrow = lax.broadcasted_iota(jnp.int32, (n, n), 0)
    ncol = lax.broadcasted_iota(jnp.int32, (n, n), 1)
    eye_n = (nrow == ncol).astype(BF16)
    lane_c = lax.broadcasted_iota(jnp.int32, (c, pair_w), 1) < SSM_HEAD_DIM
    lane_n = lax.broadcasted_iota(jnp.int32, (n, pair_w), 1) < SSM_HEAD_DIM
    lane_1 = lax.broadcasted_iota(jnp.int32, (1, pair_w), 1) < SSM_HEAD_DIM

    heads_per_group = SSM_HEADS // SSM_GROUPS
    for g in range(SSM_GROUPS):
        b_lo = SSM_INNER + g * n
        c_lo = SSM_INNER + SSM_GROUPS * n + g * n
        bm = act[:, b_lo:b_lo + n].astype(BF16)
        cm = act[:, c_lo:c_lo + n]
        cb = lax.dot_general(cm.astype(BF16), bm, NT_DIMS, preferred_element_type=F32)
        bm_t = lax.dot_general(eye_n, bm, NT_DIMS, preferred_element_type=F32).astype(BF16)
        for pair in range(heads_per_group // 2):
            ha = g * heads_per_group + 2 * pair
            hb = ha + 1
            lo = ha * SSM_HEAD_DIM
            x_pair = act[:, lo:lo + pair_w]
            st = ht_ref[0, :, lo:lo + pair_w]
            dt_pair = jnp.where(lane_c, dt[:, ha:ha + 1], dt[:, hb:hb + 1])
            dte_pair = jnp.where(lane_c, dt_end[:, ha:ha + 1], dt_end[:, hb:hb + 1])
            rhs = jnp.concatenate([(x_pair * dt_pair).astype(BF16), st.astype(BF16)], axis=0)
            ys = []
            for h in (ha, hb):
                seg = cum[:, h:h + 1] - cum_t[h:h + 1, :]
                decay = jnp.exp(jnp.where(tril, seg, -jnp.inf))
                lhs = jnp.concatenate(
                    [(cb * decay).astype(BF16), (cm * exp_cum[:, h:h + 1]).astype(BF16)], axis=1)
                ys.append(jnp.dot(lhs, rhs, preferred_element_type=F32))
            y_ref[:, lo:lo + pair_w] = jnp.where(lane_c, ys[0], ys[1])
            upd = jnp.dot(bm_t, (x_pair * dte_pair).astype(BF16), preferred_element_type=F32)
            cd_pair = jnp.where(lane_1, chunk_decay[:, ha:ha + 1], chunk_decay[:, hb:hb + 1])
            ht_ref[0, :, lo:lo + pair_w] = st * cd_pair + upd

    y = y_ref[...] + dskip_ref[...] * act[:, :SSM_INNER]
    zz = z_ref[0]
    gated = y * (zz * _sigmoid(zz))
    yn_ref[0] = ((gated * _rms_scale(gated)) * norm_ref[...]).astype(yn_ref.dtype)


def _ssd(xbc, dt, z, ctx8, h0t, conv_w, conv_b, dt_bias, a_log, d_skip, ssm_norm, c):
    b, l, _ = xbc.shape
    assert l % c == 0 and c % SUBLANES_V7X == 0
    n = SSM_STATE
    per_b = lambda shape: pl.BlockSpec(shape, lambda i, j: (i, 0, 0))
    chunk = lambda w: pl.BlockSpec((1, c, w), lambda i, j: (i, j, 0))
    full = lambda shape: pl.BlockSpec(shape, lambda i, j: (0, 0))
    block_bytes = (_nbytes((c, CONV_DIM), F32) + _nbytes((c, LANES_V7X), F32) + _nbytes((c, SSM_INNER), F32)
                   + _nbytes((8, CONV_DIM), F32) + 2 * _nbytes((n, SSM_INNER), F32)
                   + _nbytes((c, SSM_INNER), BF16) + 8 * _nbytes((8, CONV_DIM), F32))
    scratch_bytes = _nbytes((c + 8, CONV_DIM), F32) + _nbytes((c, SSM_INNER), F32)
    return pl.pallas_call(
        functools.partial(_ssd_kernel, c=c),
        grid=(b, l // c),
        in_specs=[chunk(CONV_DIM), chunk(SSM_HEADS), chunk(SSM_INNER), per_b((1, 8, CONV_DIM)),
                  per_b((1, n, SSM_INNER)), full((CONV_W, CONV_DIM)), full((1, CONV_DIM)),
                  full((1, SSM_HEADS)), full((1, SSM_HEADS)), full((1, SSM_INNER)), full((1, SSM_INNER))],
        out_specs=[chunk(SSM_INNER), per_b((1, n, SSM_INNER))],
        out_shape=[jax.ShapeDtypeStruct((b, l, SSM_INNER), BF16),
                   jax.ShapeDtypeStruct((b, n, SSM_INNER), F32)],
        scratch_shapes=[pltpu.VMEM((c + 8, CONV_DIM), F32), pltpu.VMEM((c, SSM_INNER), F32)],
        compiler_params=pltpu.CompilerParams(
            dimension_semantics=("parallel", "arbitrary"),
            vmem_limit_bytes=_vmem_limit(block_bytes, scratch_bytes)),
        name="ssd",
    )(xbc, dt, z, ctx8, h0t, conv_w, conv_b.reshape(1, -1), dt_bias.reshape(1, -1), a_log.reshape(1, -1),
      jnp.repeat(d_skip, SSM_HEAD_DIM).reshape(1, -1), ssm_norm.reshape(1, -1))


def _sb_attn_kernel(*refs, tq, tkp, n_past, scale):
    if n_past:
        q_ref, kn_ref, vn_ref, kp_ref, vp_ref, o_ref, acc_ref, run_ref = refs
    else:
        q_ref, kn_ref, vn_ref, o_ref, acc_ref, run_ref = refs
    qi = pl.program_id(2)
    pair_w = 2 * SB_HEAD_DIM
    q = q_ref[0]
    lane = lax.broadcasted_iota(jnp.int32, (tq, pair_w), 1) < SB_HEAD_DIM
    zero = jnp.zeros_like(q)
    q_heads = (jnp.where(lane, q, zero), jnp.where(lane, zero, q))

    def suffix_ones(t):
        r = lax.broadcasted_iota(jnp.int32, (t, t), 0)
        s = lax.broadcasted_iota(jnp.int32, (t, t), 1)
        return (r >= s).astype(BF16)

    acc_ref[...] = jnp.zeros_like(acc_ref)
    run_ref[...] = jnp.zeros_like(run_ref)

    def visit(k, v, ones, mask):
        for h in range(2):
            zs = lax.dot_general(q_heads[h], k, NT_DIMS, preferred_element_type=F32) * scale
            log_keep = -(jnp.maximum(zs, 0.0) + jnp.log1p(jnp.exp(-jnp.abs(zs))))
            if mask is not None:
                log_keep = jnp.where(mask, log_keep, 0.0)
            hi = log_keep.astype(BF16)
            lo = (log_keep - hi.astype(F32)).astype(BF16)
            suffix = (jnp.dot(hi, ones, preferred_element_type=F32)
                      + jnp.dot(lo, ones, preferred_element_type=F32))
            run = run_ref[h]
            w = jnp.exp(zs + suffix + run)
            if mask is not None:
                w = jnp.where(mask, w, 0.0)
            acc_ref[h] += jnp.dot(w.astype(BF16), v, preferred_element_type=F32)
            run_ref[h] = run + suffix[:, 0:1]

    ones_q = suffix_ones(tq)
    r = lax.broadcasted_iota(jnp.int32, (tq, tq), 0)
    s = lax.broadcasted_iota(jnp.int32, (tq, tq), 1)
    d0 = pl.multiple_of(qi * tq, tq)
    visit(kn_ref[0, pl.ds(d0, tq), :], vn_ref[0, pl.ds(d0, tq), :], ones_q, s < r)

    def new_body(i, carry):
        off = pl.multiple_of((qi - 1 - i) * tq, tq)
        visit(kn_ref[0, pl.ds(off, tq), :], vn_ref[0, pl.ds(off, tq), :], ones_q, None)
        return carry

    lax.fori_loop(0, qi, new_body, 0)

    if n_past:
        ones_p = suffix_ones(tkp)

        def past_body(i, carry):
            off = pl.multiple_of((n_past - 1 - i) * tkp, tkp)
            visit(kp_ref[0, pl.ds(off, tkp), :].astype(BF16), vp_ref[0, pl.ds(off, tkp), :].astype(BF16),
                  ones_p, None)
            return carry

        lax.fori_loop(0, n_past, past_body, 0)

    o_ref[0] = jnp.where(lane, acc_ref[0], acc_ref[1]).astype(o_ref.dtype)


def _sb_attn(q, k_new, v_new, k_past, v_past, tq, tkp):
    b, l, _ = q.shape
    assert l % tq == 0
    pair_w = 2 * SB_HEAD_DIM
    n_pairs = SB_INNER // pair_w
    n_past = 0
    tile = pl.BlockSpec((1, tq, pair_w), lambda i, hp, qi: (i, qi, hp))
    whole = lambda length: pl.BlockSpec((1, length, pair_w), lambda i, hp, qi: (i, 0, hp))
    in_specs = [tile, whole(l), whole(l)]
    args = [q, k_new, v_new]
    block_bytes = 2 * _nbytes((tq, pair_w), BF16) + 2 * _nbytes((l, pair_w), BF16)
    if k_past is not None:
        p = k_past.shape[1]
        assert p % tkp == 0
        n_past = p // tkp
        in_specs += [whole(p), whole(p)]
        args += [k_past, v_past]
        block_bytes += 2 * _nbytes((p, pair_w), F32)
    scratch_bytes = 2 * _nbytes((tq, pair_w), F32) * 2
    return pl.pallas_call(
        functools.partial(_sb_attn_kernel, tq=tq, tkp=tkp, n_past=n_past, scale=SB_HEAD_DIM ** -0.5),
        grid=(b, n_pairs, l // tq),
        in_specs=in_specs,
        out_specs=tile,
        out_shape=jax.ShapeDtypeStruct((b, l, SB_INNER), BF16),
        scratch_shapes=[pltpu.VMEM((2, tq, pair_w), F32), pltpu.VMEM((2, tq, 1), F32)],
        compiler_params=pltpu.CompilerParams(
            dimension_semantics=("parallel", "parallel", "arbitrary"),
            vmem_limit_bytes=_vmem_limit(block_bytes, scratch_bytes)),
        name="sb_attn",
    )(*args)


def _merge_mem_kernel(x_ref, yn_ref, ysb_ref, gs_ref, gb_ref, mk_ref, mv_ref, wssm_ref, wsb_ref, wout_ref,
                      nq_ref, wq_ref, wo_ref, o_ref, *, scale):
    y_ssm = jnp.dot(yn_ref[0], wssm_ref[...], preferred_element_type=F32)
    y_sb = jnp.dot(ysb_ref[0], wsb_ref[...], preferred_element_type=F32)
    merged = _sigmoid(gs_ref[0]) * y_ssm + _sigmoid(gb_ref[0]) * y_sb
    x1 = x_ref[0] + jnp.dot(merged.astype(BF16), wout_ref[...], preferred_element_type=F32)
    hq = ((x1 * _rms_scale(x1)) * nq_ref[...]).astype(BF16)
    q = jnp.dot(hq, wq_ref[...], preferred_element_type=F32)
    outs = []
    for h in range(MEM_HEADS):
        sl = slice(h * MEM_HEAD_DIM, (h + 1) * MEM_HEAD_DIM)
        mk = mk_ref[0, :, sl].astype(BF16)
        mv = mv_ref[0, :, sl].astype(BF16)
        s = lax.dot_general(q[:, sl].astype(BF16), mk, NT_DIMS, preferred_element_type=F32) * scale
        e = jnp.exp(s - jnp.max(s, axis=-1, keepdims=True))
        p = e / jnp.sum(e, axis=-1, keepdims=True)
        outs.append(jnp.dot(p.astype(BF16), mv, preferred_element_type=F32))
    o = jnp.concatenate(outs, axis=1)
    o_ref[0] = x1 + jnp.dot(o.astype(BF16), wo_ref[...], preferred_element_type=F32)


def _merge_mem(x, yn, ysb, gs, gb, mk, mv, w_ssm, w_sb, w_out, norm_q, w_q, w_o, tm):
    b, l, d = x.shape
    m = mk.shape[1]
    assert l % tm == 0
    tok = lambda w: pl.BlockSpec((1, tm, w), lambda i, j: (i, j, 0))
    per_b = pl.BlockSpec((1, m, d), lambda i, j: (i, 0, 0))
    full = lambda a: pl.BlockSpec(a.shape, lambda i, j: (0, 0))
    nq = norm_q.reshape(1, d)
    weights = (w_ssm, w_sb, w_out, nq, w_q, w_o)
    block_bytes = (4 * _nbytes((tm, d), F32) + _nbytes((tm, SSM_INNER + SB_INNER), BF16)
                   + 2 * _nbytes((m, d), F32) + sum(_nbytes(w.shape, w.dtype) for w in weights))
    return pl.pallas_call(
        functools.partial(_merge_mem_kernel, scale=MEM_HEAD_DIM ** -0.5),
        grid=(b, l // tm),
        in_specs=[tok(d), tok(SSM_INNER), tok(SB_INNER), tok(d), tok(d), per_b, per_b] + [full(w) for w in weights],
        out_specs=tok(d),
        out_shape=jax.ShapeDtypeStruct((b, l, d), F32),
        compiler_params=pltpu.CompilerParams(
            dimension_semantics=("parallel", "parallel"), vmem_limit_bytes=_vmem_limit(block_bytes)),
        name="merge_mem",
    )(x, yn, ysb, gs, gb, mk, mv, *weights)


def _route(logits):
    lane = lax.broadcasted_iota(jnp.int32, logits.shape, 1)
    big = jnp.int32(LANES_V7X)
    neg = -jnp.inf
    is_group = lane < MOE_GROUPS
    gl = jnp.where(is_group, logits, neg)
    ge = jnp.where(is_group, jnp.exp(gl - jnp.max(gl, axis=-1, keepdims=True)), 0.0)
    pg = ge / jnp.sum(ge, axis=-1, keepdims=True)
    p_sel = jnp.max(pg, axis=-1, keepdims=True)
    g_sel = jnp.min(jnp.where(is_group & (pg == p_sel), lane, big), axis=-1, keepdims=True)
    e_idx = lane - MOE_GROUPS
    in_group = (e_idx >= 0) & (e_idx < N_EXPERTS) & ((e_idx // EXPERTS_PER_GROUP) == g_sel)
    le = jnp.where(in_group, logits, neg)
    v1 = jnp.max(le, axis=-1, keepdims=True)
    i1 = jnp.min(jnp.where(in_group & (le == v1), lane, big), axis=-1, keepdims=True)
    rest = in_group & (lane != i1)
    le2 = jnp.where(rest, logits, neg)
    v2 = jnp.max(le2, axis=-1, keepdims=True)
    i2 = jnp.min(jnp.where(rest & (le2 == v2), lane, big), axis=-1, keepdims=True)
    e2 = jnp.exp(v2 - v1)
    den = 1.0 + e2
    return (jnp.where(lane == i1, (1.0 / den) * p_sel, 0.0)
            + jnp.where(lane == i2, (e2 / den) * p_sel, 0.0))


def _moe_kernel(x_ref, nf_ref, wr_ref, wg_ref, wu_ref, wd_ref, nfin_ref, y_ref, h_ref, dw_ref, acc_ref, *,
                final_norm):
    e = pl.program_id(1)

    @pl.when(e == 0)
    def _():
        x = x_ref[...]
        hb = ((x * _rms_scale(x)) * nf_ref[...]).astype(BF16)
        h_ref[...] = hb
        dw_ref[...] = _route(jnp.dot(hb, wr_ref[...], preferred_element_type=F32))
        acc_ref[...] = jnp.zeros_like(acc_ref)

    hb = h_ref[...]
    gate = jnp.dot(hb, wg_ref[0], preferred_element_type=F32)
    up = jnp.dot(hb, wu_ref[0], preferred_element_type=F32)
    he = (gate * _sigmoid(gate)) * up
    down = jnp.dot(he.astype(BF16), wd_ref[0], preferred_element_type=F32)
    lane = lax.broadcasted_iota(jnp.int32, dw_ref.shape, 1)
    dcol = jnp.sum(jnp.where(lane == e + MOE_GROUPS, dw_ref[...], 0.0), axis=-1, keepdims=True)
    acc_ref[...] += dcol * down

    @pl.when(e == pl.num_programs(1) - 1)
    def _():
        xo = x_ref[...] + acc_ref[...]
        y_ref[...] = (xo * _rms_scale(xo)) * nfin_ref[...] if final_norm else xo


def _moe(x2d, norm_ffn, w_router, w_gate, w_up, w_down, norm_final, final_norm, tm):
    t, d = x2d.shape
    assert t % tm == 0
    n_e, _, ff = w_gate.shape
    row = lambda a: pl.BlockSpec(a.shape, lambda i, e: (0, 0))
    nf, nfin = norm_ffn.reshape(1, d), norm_final.reshape(1, d)
    block_bytes = (2 * _nbytes((tm, d), F32) + 2 * _nbytes((1, d), F32) + _nbytes(w_router.shape, BF16)
                   + 3 * _nbytes((d, ff), BF16))
    scratch_bytes = _nbytes((tm, d), BF16) + _nbytes((tm, LANES_V7X), F32) + _nbytes((tm, d), F32)
    return pl.pallas_call(
        functools.partial(_moe_kernel, final_norm=final_norm),
        grid=(t // tm, n_e),
        in_specs=[pl.BlockSpec((tm, d), lambda i, e: (i, 0)), row(nf), row(w_router),
                  pl.BlockSpec((1, d, ff), lambda i, e: (e, 0, 0)),
                  pl.BlockSpec((1, d, ff), lambda i, e: (e, 0, 0)),
                  pl.BlockSpec((1, ff, d), lambda i, e: (e, 0, 0)), row(nfin)],
        out_specs=pl.BlockSpec((tm, d), lambda i, e: (i, 0)),
        out_shape=jax.ShapeDtypeStruct((t, d), F32),
        scratch_shapes=[pltpu.VMEM((tm, d), BF16), pltpu.VMEM((tm, LANES_V7X), F32), pltpu.VMEM((tm, d), F32)],
        compiler_params=pltpu.CompilerParams(
            dimension_semantics=("parallel", "arbitrary"),
            vmem_limit_bytes=_vmem_limit(block_bytes, scratch_bytes)),
        name="moe",
    )(x2d, nf, w_router, w_gate, w_up, w_down, nfin)


def _tile(n, pref):
    t = min(n, pref)
    while n % t:
        t //= 2
    return t


def _split_w_in(w_in):
    offs = [0]
    for s in IN_SIZES:
        offs.append(offs[-1] + s)
    return [w_in[:, offs[i]:offs[i + 1]].astype(BF16) for i in range(len(IN_SIZES))]


def _group_step(x, conv_ctx, ssm_h0, past_k, past_v, mem_k, mem_v, lw):
    b, l, d = x.shape
    t = b * l
    x2d = x.reshape(t, d)
    w_z, w_xbc, w_dt, w_q, w_k, w_v, w_gs, w_gb = lw["w_in_parts"]
    z, xbc, dt, q, k, kb, v, vb, gs, gb = _norm_proj(
        x2d, lw["norm_mix"], [w_z, w_xbc, w_dt, w_q, w_k, w_v, w_gs, w_gb],
        [(F32,), (F32,), (F32,), (BF16,), (F32, BF16), (F32, BF16), (F32,), (F32,)],
        _tile(t, 256), "in_proj")
    r3 = lambda a: a.reshape(b, l, a.shape[-1])

    h0t = ssm_h0.transpose(0, 3, 1, 2).reshape(b, SSM_STATE, SSM_INNER)
    ctx8 = jnp.pad(conv_ctx, ((0, 0), (SUBLANES_V7X - (CONV_W - 1), 0), (0, 0)))
    yn, ht = _ssd(r3(xbc), r3(dt), r3(z), ctx8, h0t, lw["conv_w"], lw["conv_b"], lw["dt_bias"], lw["a_log"],
                  lw["d_skip"], lw["ssm_norm"], _tile(l, 128))
    ssm_new = ht.reshape(b, SSM_STATE, SSM_HEADS, SSM_HEAD_DIM).transpose(0, 2, 3, 1)
    assert l >= CONV_W - 1
    conv_new = r3(xbc)[:, l - (CONV_W - 1):, :]

    ysb = _sb_attn(r3(q), r3(kb), r3(vb), past_k, past_v, _tile(l, 256), 256)

    x_mem = _merge_mem(x, yn, ysb, r3(gs), r3(gb), mem_k, mem_v, lw["w_ssm_br"], lw["w_sb_br"], lw["w_out"],
                       lw["norm_mem_q"], lw["w_mem_q"], lw["w_mem_o"], _tile(l, 256))
    kh = k.reshape(b, l, SB_HEADS, SB_HEAD_DIM)
    vh = v.reshape(b, l, SB_HEADS, SB_HEAD_DIM)
    return x_mem, conv_new, ssm_new, kh, vh


def kernel(x_prompt, x_sample, cache_sb_k, cache_sb_v, state_ssm, state_conv, cache_mem_k, cache_mem_v, mem_prompt, norm_mix, w_in, conv_w, conv_b, dt_bias, a_log, d_skip, ssm_norm, w_ssm_br, w_sb_br, w_out, norm_mem_q, norm_mem_kv, w_mem_q, w_mem_kv, w_mem_o, norm_ffn, w_group_router, w_expert_router, w_gate_e, w_up_e, w_down_e, norm_final):
    depth = norm_mix.shape[0]
    bp, lp, d = x_prompt.shape
    bs, ls, _ = x_sample.shape
    xp, xs = x_prompt, x_sample
    outs = {name: [] for name in ("skp", "svp", "ssp", "scp", "mkp", "mvp", "sks", "svs", "sss", "scs")}
    for l in range(depth):
        router = jnp.concatenate([w_group_router[l], w_expert_router[l]], axis=1)
        router = jnp.pad(router, ((0, 0), (0, LANES_V7X - router.shape[1]))).astype(BF16)
        lw = dict(
            norm_mix=norm_mix[l], w_in_parts=_split_w_in(w_in[l]), conv_w=conv_w[l], conv_b=conv_b[l],
            dt_bias=dt_bias[l], a_log=a_log[l], d_skip=d_skip[l], ssm_norm=ssm_norm[l],
            w_ssm_br=w_ssm_br[l].astype(BF16), w_sb_br=w_sb_br[l].astype(BF16), w_out=w_out[l].astype(BF16),
            norm_mem_q=norm_mem_q[l], w_mem_q=w_mem_q[l].astype(BF16), w_mem_o=w_mem_o[l].astype(BF16))
        moe_w = (norm_ffn[l], router, w_gate_e[l].astype(BF16), w_up_e[l].astype(BF16), w_down_e[l].astype(BF16))
        last = l == depth - 1

        m = mem_prompt.shape[1]
        w_kv = w_mem_kv[l].astype(BF16)
        mk_p, mv_p = _norm_proj(mem_prompt.reshape(bp * m, d), norm_mem_kv[l], [w_kv[:, :d], w_kv[:, d:]],
                                [(F32,), (F32,)], _tile(bp * m, 256), "mem_kv")
        mk_p, mv_p = mk_p.reshape(bp, m, d), mv_p.reshape(bp, m, d)
        conv0 = jnp.zeros((bp, CONV_W - 1, CONV_DIM), xp.dtype)
        h0 = jnp.zeros((bp, SSM_HEADS, SSM_HEAD_DIM, SSM_STATE), xp.dtype)
        xp, conv_p, ssm_p, k_p, v_p = _group_step(xp, conv0, h0, None, None, mk_p, mv_p, lw)
        xp = _moe(xp.reshape(bp * lp, d), *moe_w, norm_final, last, _tile(bp * lp, 1024)).reshape(bp, lp, d)
        outs["skp"].append(k_p); outs["svp"].append(v_p); outs["ssp"].append(ssm_p); outs["scp"].append(conv_p)
        outs["mkp"].append(mk_p.reshape(bp, m, MEM_HEADS, MEM_HEAD_DIM))
        outs["mvp"].append(mv_p.reshape(bp, m, MEM_HEADS, MEM_HEAD_DIM))

        past = cache_sb_k.shape[2]
        ms = cache_mem_k.shape[2]
        xs, conv_s, ssm_s, k_s, v_s = _group_step(
            xs, state_conv[l], state_ssm[l], cache_sb_k[l].reshape(bs, past, SB_INNER),
            cache_sb_v[l].reshape(bs, past, SB_INNER), cache_mem_k[l].reshape(bs, ms, d),
            cache_mem_v[l].reshape(bs, ms, d), lw)
        xs = _moe(xs.reshape(bs * ls, d), *moe_w, norm_final, last, _tile(bs * ls, 1024)).reshape(bs, ls, d)
        outs["sks"].append(k_s); outs["svs"].append(v_s); outs["sss"].append(ssm_s); outs["scs"].append(conv_s)

    st = lambda name: jnp.stack(outs[name])
    return (xp, xs, st("skp"), st("svp"), st("ssp"), st("scp"), st("mkp"), st("mvp"),
            st("sks"), st("svs"), st("sss"), st("scs"))
```

```python
import functools

import jax
import jax.numpy as jnp
from jax import lax
from jax.experimental import pallas as pl
from jax.experimental.pallas import tpu as pltpu

F32 = jnp.float32
BF16 = jnp.bfloat16
EPS = 1e-6

D_MODEL = 1024
SSM_HEADS = 16
SSM_HEAD_DIM = 64
SSM_INNER = SSM_HEADS * SSM_HEAD_DIM
SSM_GROUPS = 2
SSM_STATE = 128
CONV_W = 4
CONV_DIM = SSM_INNER + 2 * SSM_GROUPS * SSM_STATE
SB_HEADS = 8
SB_HEAD_DIM = 64
SB_INNER = SB_HEADS * SB_HEAD_DIM
MEM_HEADS = 4
MEM_HEAD_DIM = D_MODEL // MEM_HEADS
MOE_GROUPS = 4
EXPERTS_PER_GROUP = 4
N_EXPERTS = MOE_GROUPS * EXPERTS_PER_GROUP
IN_SIZES = (SSM_INNER, CONV_DIM, SSM_HEADS, SB_INNER, SB_INNER, SB_INNER, D_MODEL, D_MODEL)

LANES_V7X = 128
SUBLANES_V7X = 8
VMEM_BYTES_V7X = 64 * 1024 * 1024
COMPILER_TEMP_BYTES = 20 * 1024 * 1024
VMEM_CAP_BYTES = 60000 * 1024

NT_DIMS = (((1,), (1,)), ((), ()))

SB_CHUNK = 512
SB_SUB = 256
SB_STACK_ROWS = 1024


def _vmem_limit(block_bytes, scratch_bytes=0):
    need = 2 * block_bytes + scratch_bytes + COMPILER_TEMP_BYTES
    return int(min(need, VMEM_CAP_BYTES))


def _nbytes(shape, dtype):
    n = 1
    for s in shape:
        n *= s
    return n * jnp.dtype(dtype).itemsize


def _sigmoid(x):
    return 1.0 / (1.0 + jnp.exp(-x))


def _softplus(x):
    return jnp.maximum(x, 0.0) + jnp.log1p(jnp.exp(-jnp.abs(x)))


def _rms_scale(x):
    return lax.rsqrt(jnp.mean(x * x, axis=-1, keepdims=True) + EPS)


def _norm_proj_kernel(x_ref, g_ref, *refs, out_plan):
    n_w = len(out_plan)
    w_refs = refs[:n_w]
    o_refs = refs[n_w:]
    x = x_ref[...]
    h = ((x * _rms_scale(x)) * g_ref[...]).astype(BF16)
    oi = 0
    for w_ref, dtypes in zip(w_refs, out_plan):
        y = jnp.dot(h, w_ref[...], preferred_element_type=F32)
        for dt in dtypes:
            o_refs[oi][...] = y.astype(dt)
            oi += 1


def _norm_proj(x2d, gain, weights, out_plan, tm, name):
    t, d = x2d.shape
    assert t % tm == 0
    in_specs = [pl.BlockSpec((tm, d), lambda i: (i, 0)), pl.BlockSpec((1, d), lambda i: (0, 0))]
    block_bytes = _nbytes((tm, d), F32) + _nbytes((1, d), F32)
    for w in weights:
        in_specs.append(pl.BlockSpec(w.shape, lambda i: (0, 0)))
        block_bytes += _nbytes(w.shape, w.dtype)
    out_shapes, out_specs = [], []
    for w, dtypes in zip(weights, out_plan):
        n = w.shape[1]
        for dt in dtypes:
            out_shapes.append(jax.ShapeDtypeStruct((t, n), dt))
            out_specs.append(pl.BlockSpec((tm, n), lambda i: (i, 0)))
            block_bytes += _nbytes((tm, max(n, LANES_V7X)), dt)
    return pl.pallas_call(
        functools.partial(_norm_proj_kernel, out_plan=tuple(out_plan)),
        grid=(t // tm,),
        in_specs=in_specs,
        out_specs=out_specs,
        out_shape=out_shapes,
        compiler_params=pltpu.CompilerParams(
            dimension_semantics=("parallel",), vmem_limit_bytes=_vmem_limit(block_bytes)),
        name=name,
    )(x2d, gain.reshape(1, d), *weights)


def _ssd_kernel(xbc_ref, dt_ref, z_ref, ctx_ref, h0_ref, convw_ref, convb_ref, dtb_ref, alog_ref,
                dskip_ref, norm_ref, yn_ref, ht_ref, ext_ref, y_ref, *, c):
    j = pl.program_id(1)
    halo = SUBLANES_V7X
    n = SSM_STATE
    pair_w = 2 * SSM_HEAD_DIM

    @pl.when(j == 0)
    def _():
        ext_ref[0:halo, :] = ctx_ref[0]
        ht_ref[0] = h0_ref[0]

    raw = xbc_ref[0]
    ext_ref[halo:halo + c, :] = raw
    conv = convb_ref[...]
    for tap in range(CONV_W - 1):
        off = halo - (CONV_W - 1) + tap
        conv = conv + ext_ref[off:off + c, :] * convw_ref[tap:tap + 1, :]
    conv = conv + raw * convw_ref[CONV_W - 1:CONV_W, :]
    ext_ref[0:halo, :] = ext_ref[c:c + halo, :]
    act = conv * _sigmoid(conv)

    dt = _softplus(dt_ref[0] + dtb_ref[...])
    a = -jnp.exp(alog_ref[...])
    row = lax.broadcasted_iota(jnp.int32, (c, c), 0)
    col = lax.broadcasted_iota(jnp.int32, (c, c), 1)
    tril = row >= col
    cum = jnp.dot(tril.astype(F32), dt * a, precision=lax.Precision.HIGHEST,
                  preferred_element_type=F32)
    hrow = lax.broadcasted_iota(jnp.int32, (SSM_HEADS, SSM_HEADS), 0)
    hcol = lax.broadcasted_iota(jnp.int32, (SSM_HEADS, SSM_HEADS), 1)
    eye_h = (hrow == hcol).astype(F32)
    cum_t = lax.dot_general(eye_h, cum, NT_DIMS, precision=lax.Precision.HIGHEST,
                            preferred_element_type=F32)
    cum_last = cum[c - 1:c, :]
    exp_cum = jnp.exp(cum)
    dt_end = dt * jnp.exp(cum_last - cum)
    chunk_decay = jnp.exp(cum_last)

    nrow = lax.broadcasted_iota(jnp.int32, (n, n), 0)
    ncol = lax.broadcasted_iota(jnp.int32, (n, n), 1)
    eye_n = (nrow == ncol).astype(BF16)
    lane_c = lax.broadcasted_iota(jnp.int32, (c, pair_w), 1) < SSM_HEAD_DIM
    lane_n = lax.broadcasted_iota(jnp.int32, (n, pair_w), 1) < SSM_HEAD_DIM
    lane_1 = lax.broadcasted_iota(jnp.int32, (1, pair_w), 1) < SSM_HEAD_DIM

    heads_per_group = SSM_HEADS // SSM_GROUPS
    for g in range(SSM_GROUPS):
        b_lo = SSM_INNER + g * n
        c_lo = SSM_INNER + SSM_GROUPS * n + g * n
        bm = act[:, b_lo:b_lo + n].astype(BF16)
        cm = act[:, c_lo:c_lo + n]
        cb = lax.dot_general(cm.astype(BF16), bm, NT_DIMS, preferred_element_type=F32)
        bm_t = lax.dot_general(eye_n, bm, NT_DIMS, preferred_element_type=F32).astype(BF16)
        for pair in range(heads_per_group // 2):
            ha = g * heads_per_group + 2 * pair
            hb = ha + 1
            lo = ha * SSM_HEAD_DIM
            x_pair = act[:, lo:lo + pair_w]
            st = ht_ref[0, :, lo:lo + pair_w]
            dt_pair = jnp.where(lane_c, dt[:, ha:ha + 1], dt[:, hb:hb + 1])
            dte_pair = jnp.where(lane_c, dt_end[:, ha:ha + 1], dt_end[:, hb:hb + 1])
            rhs = jnp.concatenate([(x_pair * dt_pair).astype(BF16), st.astype(BF16)], axis=0)
            ys = []
            for h in (ha, hb):
                seg = cum[:, h:h + 1] - cum_t[h:h + 1, :]
                decay = jnp.exp(jnp.where(tril, seg, -jnp.inf))
                lhs = jnp.concatenate(
                    [(cb * decay).astype(BF16), (cm * exp_cum[:, h:h + 1]).astype(BF16)], axis=1)
                ys.append(jnp.dot(lhs, rhs, preferred_element_type=F32))
            y_ref[:, lo:lo + pair_w] = jnp.where(lane_c, ys[0], ys[1])
            upd = jnp.dot(bm_t, (x_pair * dte_pair).astype(BF16), preferred_element_type=F32)
            cd_pair = jnp.where(lane_1, chunk_decay[:, ha:ha + 1], chunk_decay[:, hb:hb + 1])
            ht_ref[0, :, lo:lo + pair_w] = st * cd_pair + upd

    y = y_ref[...] + dskip_ref[...] * act[:, :SSM_INNER]
    zz = z_ref[0]
    gated = y * (zz * _sigmoid(zz))
    yn_ref[0] = ((gated * _rms_scale(gated)) * norm_ref[...]).astype(yn_ref.dtype)


def _ssd(xbc, dt, z, ctx8, h0t, conv_w, conv_b, dt_bias, a_log, d_skip, ssm_norm, c):
    b, l, _ = xbc.shape
    assert l % c == 0 and c % SUBLANES_V7X == 0
    n = SSM_STATE
    per_b = lambda shape: pl.BlockSpec(shape, lambda i, j: (i, 0, 0))
    chunk = lambda w: pl.BlockSpec((1, c, w), lambda i, j: (i, j, 0))
    full = lambda shape: pl.BlockSpec(shape, lambda i, j: (0, 0))
    block_bytes = (_nbytes((c, CONV_DIM), F32) + _nbytes((c, LANES_V7X), F32) + _nbytes((c, SSM_INNER), F32)
                   + _nbytes((8, CONV_DIM), F32) + 2 * _nbytes((n, SSM_INNER), F32)
                   + _nbytes((c, SSM_INNER), BF16) + 8 * _nbytes((8, CONV_DIM), F32))
    scratch_bytes = _nbytes((c + 8, CONV_DIM), F32) + _nbytes((c, SSM_INNER), F32)
    return pl.pallas_call(
        functools.partial(_ssd_kernel, c=c),
        grid=(b, l // c),
        in_specs=[chunk(CONV_DIM), chunk(SSM_HEADS), chunk(SSM_INNER), per_b((1, 8, CONV_DIM)),
                  per_b((1, n, SSM_INNER)), full((CONV_W, CONV_DIM)), full((1, CONV_DIM)),
                  full((1, SSM_HEADS)), full((1, SSM_HEADS)), full((1, SSM_INNER)), full((1, SSM_INNER))],
        out_specs=[chunk(SSM_INNER), per_b((1, n, SSM_INNER))],
        out_shape=[jax.ShapeDtypeStruct((b, l, SSM_INNER), BF16),
                   jax.ShapeDtypeStruct((b, n, SSM_INNER), F32)],
        scratch_shapes=[pltpu.VMEM((c + 8, CONV_DIM), F32), pltpu.VMEM((c, SSM_INNER), F32)],
        compiler_params=pltpu.CompilerParams(
            dimension_semantics=("parallel", "arbitrary"),
            vmem_limit_bytes=_vmem_limit(block_bytes, scratch_bytes)),
        name="ssd",
    )(xbc, dt, z, ctx8, h0t, conv_w, conv_b.reshape(1, -1), dt_bias.reshape(1, -1), a_log.reshape(1, -1),
      jnp.repeat(d_skip, SSM_HEAD_DIM).reshape(1, -1), ssm_norm.reshape(1, -1))


def _sb_attn_kernel(*refs, tq, tkp, n_past, hps, scale):
    if n_past:
        q_ref, kn_ref, vn_ref, kp_ref, vp_ref, o_ref, acc_ref, run_ref = refs
    else:
        q_ref, kn_ref, vn_ref, o_ref, acc_ref, run_ref = refs
    qi = pl.program_id(2)
    width = hps * SB_HEAD_DIM
    m = hps * tq
    q = (q_ref[0].astype(F32) * scale).astype(BF16)
    head_of_lane = lax.broadcasted_iota(jnp.int32, (tq, width), 1) // SB_HEAD_DIM
    zero = jnp.zeros_like(q)
    q_stack = jnp.concatenate([jnp.where(head_of_lane == h, q, zero) for h in range(hps)], axis=0)

    def suffix_ones(t):
        r = lax.broadcasted_iota(jnp.int32, (t, t), 0)
        s = lax.broadcasted_iota(jnp.int32, (t, t), 1)
        return jnp.where(r > s, -1.0, 0.0).astype(BF16)

    acc_ref[...] = jnp.zeros_like(acc_ref)
    run_ref[...] = jnp.zeros_like(run_ref)
    sign_bit = jnp.int32(-2 ** 31)

    def visit(k, v, mask):
        ck = k.shape[0]
        sub = min(ck, SB_SUB)
        ones = suffix_ones(sub)
        zs = lax.dot_general(q_stack, k, NT_DIMS, preferred_element_type=F32)
        neg_abs = lax.bitcast_convert_type(lax.bitcast_convert_type(zs, jnp.int32) | sign_bit, F32)
        drop = jnp.maximum(zs, 0.0) + jnp.log(1.0 + jnp.exp(neg_abs))
        if mask is not None:
            drop = jnp.where(mask, drop, 0.0)
        drop_b = drop.astype(BF16)
        log_beta = zs - drop
        run = run_ref[...]
        ws = []
        for j in reversed(range(ck // sub)):
            sl = slice(j * sub, (j + 1) * sub)
            suffix = jnp.dot(drop_b[:, sl], ones, preferred_element_type=F32)
            w = jnp.exp(log_beta[:, sl] + suffix + run)
            if mask is not None:
                w = jnp.where(mask[:, sl], w, 0.0)
            ws.insert(0, w.astype(BF16))
            first = drop_b[:, j * sub:j * sub + 1].astype(F32)
            run = run + (suffix[:, 0:1] - first)
        w_all = ws[0] if len(ws) == 1 else jnp.concatenate(ws, axis=1)
        acc_ref[...] += jnp.dot(w_all, v, preferred_element_type=F32)
        run_ref[...] = run

    r = lax.rem(lax.broadcasted_iota(jnp.int32, (m, tq), 0), tq)
    s = lax.broadcasted_iota(jnp.int32, (m, tq), 1)
    d0 = pl.multiple_of(qi * tq, tq)
    visit(kn_ref[0, pl.ds(d0, tq), :], vn_ref[0, pl.ds(d0, tq), :], s < r)

    def new_body(i, carry):
        off = pl.multiple_of((qi - 1 - i) * tq, tq)
        visit(kn_ref[0, pl.ds(off, tq), :], vn_ref[0, pl.ds(off, tq), :], None)
        return carry

    lax.fori_loop(0, qi, new_body, 0)

    if n_past:
        def past_body(i, carry):
            off = pl.multiple_of((n_past - 1 - i) * tkp, tkp)
            visit(kp_ref[0, pl.ds(off, tkp), :].astype(BF16), vp_ref[0, pl.ds(off, tkp), :].astype(BF16), None)
            return carry

        lax.fori_loop(0, n_past, past_body, 0)

    out = acc_ref[0:tq, :]
    for h in range(1, hps):
        out = jnp.where(head_of_lane == h, acc_ref[h * tq:(h + 1) * tq, :], out)
    o_ref[0] = out.astype(o_ref.dtype)


def _sb_attn(q, k_new, v_new, k_past, v_past, tq, tkp, hps):
    b, l, _ = q.shape
    assert l % tq == 0 and SB_HEADS % hps == 0
    width = hps * SB_HEAD_DIM
    m = hps * tq
    n_past = 0
    tile = pl.BlockSpec((1, tq, width), lambda i, hq, qi: (i, qi, hq))
    whole = lambda length: pl.BlockSpec((1, length, width), lambda i, hq, qi: (i, 0, hq))
    in_specs = [tile, whole(l), whole(l)]
    args = [q, k_new, v_new]
    block_bytes = 2 * _nbytes((tq, width), BF16) + 2 * _nbytes((l, width), BF16)
    if k_past is not None:
        p = k_past.shape[1]
        assert p % tkp == 0
        n_past = p // tkp
        in_specs += [whole(p), whole(p)]
        args += [k_past, v_past]
        block_bytes += 2 * _nbytes((p, width), F32)
    scratch_bytes = _nbytes((m, width), F32) + _nbytes((m, LANES_V7X), F32)
    return pl.pallas_call(
        functools.partial(_sb_attn_kernel, tq=tq, tkp=tkp, n_past=n_past, hps=hps, scale=SB_HEAD_DIM ** -0.5),
        grid=(b, SB_HEADS // hps, l // tq),
        in_specs=in_specs,
        out_specs=tile,
        out_shape=jax.ShapeDtypeStruct((b, l, SB_INNER), BF16),
        scratch_shapes=[pltpu.VMEM((m, width), F32), pltpu.VMEM((m, 1), F32)],
        compiler_params=pltpu.CompilerParams(
            dimension_semantics=("parallel", "parallel", "arbitrary"),
            vmem_limit_bytes=_vmem_limit(block_bytes, scratch_bytes)),
        name="sb_attn",
    )(*args)


def _merge_mem_kernel(x_ref, yn_ref, ysb_ref, gs_ref, gb_ref, mk_ref, mv_ref, wssm_ref, wsb_ref, wout_ref,
                      nq_ref, wq_ref, wo_ref, o_ref, *, scale):
    y_ssm = jnp.dot(yn_ref[0], wssm_ref[...], preferred_element_type=F32)
    y_sb = jnp.dot(ysb_ref[0], wsb_ref[...], preferred_element_type=F32)
    merged = _sigmoid(gs_ref[0]) * y_ssm + _sigmoid(gb_ref[0]) * y_sb
    x1 = x_ref[0] + jnp.dot(merged.astype(BF16), wout_ref[...], preferred_element_type=F32)
    hq = ((x1 * _rms_scale(x1)) * nq_ref[...]).astype(BF16)
    q = jnp.dot(hq, wq_ref[...], preferred_element_type=F32)
    outs = []
    for h in range(MEM_HEADS):
        sl = slice(h * MEM_HEAD_DIM, (h + 1) * MEM_HEAD_DIM)
        mk = mk_ref[0, :, sl].astype(BF16)
        mv = mv_ref[0, :, sl].astype(BF16)
        s = lax.dot_general(q[:, sl].astype(BF16), mk, NT_DIMS, preferred_element_type=F32) * scale
        e = jnp.exp(s - jnp.max(s, axis=-1, keepdims=True))
        p = e / jnp.sum(e, axis=-1, keepdims=True)
        outs.append(jnp.dot(p.astype(BF16), mv, preferred_element_type=F32))
    o = jnp.concatenate(outs, axis=1)
    o_ref[0] = x1 + jnp.dot(o.astype(BF16), wo_ref[...], preferred_element_type=F32)


def _merge_mem(x, yn, ysb, gs, gb, mk, mv, w_ssm, w_sb, w_out, norm_q, w_q, w_o, tm):
    b, l, d = x.shape
    m = mk.shape[1]
    assert l % tm == 0
    tok = lambda w: pl.BlockSpec((1, tm, w), lambda i, j: (i, j, 0))
    per_b = pl.BlockSpec((1, m, d), lambda i, j: (i, 0, 0))
    full = lambda a: pl.BlockSpec(a.shape, lambda i, j: (0, 0))
    nq = norm_q.reshape(1, d)
    weights = (w_ssm, w_sb, w_out, nq, w_q, w_o)
    block_bytes = (4 * _nbytes((tm, d), F32) + _nbytes((tm, SSM_INNER + SB_INNER), BF16)
                   + 2 * _nbytes((m, d), F32) + sum(_nbytes(w.shape, w.dtype) for w in weights))
    return pl.pallas_call(
        functools.partial(_merge_mem_kernel, scale=MEM_HEAD_DIM ** -0.5),
        grid=(b, l // tm),
        in_specs=[tok(d), tok(SSM_INNER), tok(SB_INNER), tok(d), tok(d), per_b, per_b] + [full(w) for w in weights],
        out_specs=tok(d),
        out_shape=jax.ShapeDtypeStruct((b, l, d), F32),
        compiler_params=pltpu.CompilerParams(
            dimension_semantics=("parallel", "parallel"), vmem_limit_bytes=_vmem_limit(block_bytes)),
        name="merge_mem",
    )(x, yn, ysb, gs, gb, mk, mv, *weights)


def _route(logits):
    lane = lax.broadcasted_iota(jnp.int32, logits.shape, 1)
    big = jnp.int32(LANES_V7X)
    neg = -jnp.inf
    is_group = lane < MOE_GROUPS
    gl = jnp.where(is_group, logits, neg)
    ge = jnp.where(is_group, jnp.exp(gl - jnp.max(gl, axis=-1, keepdims=True)), 0.0)
    pg = ge / jnp.sum(ge, axis=-1, keepdims=True)
    p_sel = jnp.max(pg, axis=-1, keepdims=True)
    g_sel = jnp.min(jnp.where(is_group & (pg == p_sel), lane, big), axis=-1, keepdims=True)
    e_idx = lane - MOE_GROUPS
    in_group = (e_idx >= 0) & (e_idx < N_EXPERTS) & ((e_idx // EXPERTS_PER_GROUP) == g_sel)
    le = jnp.where(in_group, logits, neg)
    v1 = jnp.max(le, axis=-1, keepdims=True)
    i1 = jnp.min(jnp.where(in_group & (le == v1), lane, big), axis=-1, keepdims=True)
    rest = in_group & (lane != i1)
    le2 = jnp.where(rest, logits, neg)
    v2 = jnp.max(le2, axis=-1, keepdims=True)
    i2 = jnp.min(jnp.where(rest & (le2 == v2), lane, big), axis=-1, keepdims=True)
    e2 = jnp.exp(v2 - v1)
    den = 1.0 + e2
    return (jnp.where(lane == i1, (1.0 / den) * p_sel, 0.0)
            + jnp.where(lane == i2, (e2 / den) * p_sel, 0.0))


def _moe_kernel(x_ref, nf_ref, wr_ref, wg_ref, wu_ref, wd_ref, nfin_ref, y_ref, h_ref, dw_ref, acc_ref, *,
                final_norm):
    e = pl.program_id(1)

    @pl.when(e == 0)
    def _():
        x = x_ref[...]
        hb = ((x * _rms_scale(x)) * nf_ref[...]).astype(BF16)
        h_ref[...] = hb
        dw_ref[...] = _route(jnp.dot(hb, wr_ref[...], preferred_element_type=F32))
        acc_ref[...] = jnp.zeros_like(acc_ref)

    hb = h_ref[...]
    gate = jnp.dot(hb, wg_ref[0], preferred_element_type=F32)
    up = jnp.dot(hb, wu_ref[0], preferred_element_type=F32)
    he = (gate * _sigmoid(gate)) * up
    down = jnp.dot(he.astype(BF16), wd_ref[0], preferred_element_type=F32)
    lane = lax.broadcasted_iota(jnp.int32, dw_ref.shape, 1)
    dcol = jnp.sum(jnp.where(lane == e + MOE_GROUPS, dw_ref[...], 0.0), axis=-1, keepdims=True)
    acc_ref[...] += dcol * down

    @pl.when(e == pl.num_programs(1) - 1)
    def _():
        xo = x_ref[...] + acc_ref[...]
        y_ref[...] = (xo * _rms_scale(xo)) * nfin_ref[...] if final_norm else xo


def _moe(x2d, norm_ffn, w_router, w_gate, w_up, w_down, norm_final, final_norm, tm):
    t, d = x2d.shape
    assert t % tm == 0
    n_e, _, ff = w_gate.shape
    row = lambda a: pl.BlockSpec(a.shape, lambda i, e: (0, 0))
    nf, nfin = norm_ffn.reshape(1, d), norm_final.reshape(1, d)
    block_bytes = (2 * _nbytes((tm, d), F32) + 2 * _nbytes((1, d), F32) + _nbytes(w_router.shape, BF16)
                   + 3 * _nbytes((d, ff), BF16))
    scratch_bytes = _nbytes((tm, d), BF16) + _nbytes((tm, LANES_V7X), F32) + _nbytes((tm, d), F32)
    return pl.pallas_call(
        functools.partial(_moe_kernel, final_norm=final_norm),
        grid=(t // tm, n_e),
        in_specs=[pl.BlockSpec((tm, d), lambda i, e: (i, 0)), row(nf), row(w_router),
                  pl.BlockSpec((1, d, ff), lambda i, e: (e, 0, 0)),
                  pl.BlockSpec((1, d, ff), lambda i, e: (e, 0, 0)),
                  pl.BlockSpec((1, ff, d), lambda i, e: (e, 0, 0)), row(nfin)],
        out_specs=pl.BlockSpec((tm, d), lambda i, e: (i, 0)),
        out_shape=jax.ShapeDtypeStruct((t, d), F32),
        scratch_shapes=[pltpu.VMEM((tm, d), BF16), pltpu.VMEM((tm, LANES_V7X), F32), pltpu.VMEM((tm, d), F32)],
        compiler_params=pltpu.CompilerParams(
            dimension_semantics=("parallel", "arbitrary"),
            vmem_limit_bytes=_vmem_limit(block_bytes, scratch_bytes)),
        name="moe",
    )(x2d, nf, w_router, w_gate, w_up, w_down, nfin)


def _tile(n, pref):
    t = min(n, pref)
    while n % t:
        t //= 2
    return t


def _split_w_in(w_in):
    offs = [0]
    for s in IN_SIZES:
        offs.append(offs[-1] + s)
    return [w_in[:, offs[i]:offs[i + 1]].astype(BF16) for i in range(len(IN_SIZES))]


def _group_step(x, conv_ctx, ssm_h0, past_k, past_v, mem_k, mem_v, lw):
    b, l, d = x.shape
    t = b * l
    x2d = x.reshape(t, d)
    w_z, w_xbc, w_dt, w_q, w_k, w_v, w_gs, w_gb = lw["w_in_parts"]
    z, xbc, dt, q, k, kb, v, vb, gs, gb = _norm_proj(
        x2d, lw["norm_mix"], [w_z, w_xbc, w_dt, w_q, w_k, w_v, w_gs, w_gb],
        [(F32,), (F32,), (F32,), (BF16,), (F32, BF16), (F32, BF16), (F32,), (F32,)],
        _tile(t, 256), "in_proj")
    r3 = lambda a: a.reshape(b, l, a.shape[-1])

    h0t = ssm_h0.transpose(0, 3, 1, 2).reshape(b, SSM_STATE, SSM_INNER)
    ctx8 = jnp.pad(conv_ctx, ((0, 0), (SUBLANES_V7X - (CONV_W - 1), 0), (0, 0)))
    yn, ht = _ssd(r3(xbc), r3(dt), r3(z), ctx8, h0t, lw["conv_w"], lw["conv_b"], lw["dt_bias"], lw["a_log"],
                  lw["d_skip"], lw["ssm_norm"], _tile(l, 128))
    ssm_new = ht.reshape(b, SSM_STATE, SSM_HEADS, SSM_HEAD_DIM).transpose(0, 2, 3, 1)
    assert l >= CONV_W - 1
    conv_new = r3(xbc)[:, l - (CONV_W - 1):, :]

    tq = _tile(l, SB_CHUNK)
    hps = min(SB_HEADS, SB_STACK_ROWS // tq)
    ysb = _sb_attn(r3(q), r3(kb), r3(vb), past_k, past_v, tq, SB_CHUNK, hps)

    x_mem = _merge_mem(x, yn, ysb, r3(gs), r3(gb), mem_k, mem_v, lw["w_ssm_br"], lw["w_sb_br"], lw["w_out"],
                       lw["norm_mem_q"], lw["w_mem_q"], lw["w_mem_o"], _tile(l, 256))
    kh = k.reshape(b, l, SB_HEADS, SB_HEAD_DIM)
    vh = v.reshape(b, l, SB_HEADS, SB_HEAD_DIM)
    return x_mem, conv_new, ssm_new, kh, vh


def kernel(x_prompt, x_sample, cache_sb_k, cache_sb_v, state_ssm, state_conv, cache_mem_k, cache_mem_v, mem_prompt, norm_mix, w_in, conv_w, conv_b, dt_bias, a_log, d_skip, ssm_norm, w_ssm_br, w_sb_br, w_out, norm_mem_q, norm_mem_kv, w_mem_q, w_mem_kv, w_mem_o, norm_ffn, w_group_router, w_expert_router, w_gate_e, w_up_e, w_down_e, norm_final):
    depth = norm_mix.shape[0]
    bp, lp, d = x_prompt.shape
    bs, ls, _ = x_sample.shape
    xp, xs = x_prompt, x_sample
    outs = {name: [] for name in ("skp", "svp", "ssp", "scp", "mkp", "mvp", "sks", "svs", "sss", "scs")}
    for l in range(depth):
        router = jnp.concatenate([w_group_router[l], w_expert_router[l]], axis=1)
        router = jnp.pad(router, ((0, 0), (0, LANES_V7X - router.shape[1]))).astype(BF16)
        lw = dict(
            norm_mix=norm_mix[l], w_in_parts=_split_w_in(w_in[l]), conv_w=conv_w[l], conv_b=conv_b[l],
            dt_bias=dt_bias[l], a_log=a_log[l], d_skip=d_skip[l], ssm_norm=ssm_norm[l],
            w_ssm_br=w_ssm_br[l].astype(BF16), w_sb_br=w_sb_br[l].astype(BF16), w_out=w_out[l].astype(BF16),
            norm_mem_q=norm_mem_q[l], w_mem_q=w_mem_q[l].astype(BF16), w_mem_o=w_mem_o[l].astype(BF16))
        moe_w = (norm_ffn[l], router, w_gate_e[l].astype(BF16), w_up_e[l].astype(BF16), w_down_e[l].astype(BF16))
        last = l == depth - 1

        m = mem_prompt.shape[1]
        w_kv = w_mem_kv[l].astype(BF16)
        mk_p, mv_p = _norm_proj(mem_prompt.reshape(bp * m, d), norm_mem_kv[l], [w_kv[:, :d], w_kv[:, d:]],
                                [(F32,), (F32,)], _tile(bp * m, 256), "mem_kv")
        mk_p, mv_p = mk_p.reshape(bp, m, d), mv_p.reshape(bp, m, d)
        conv0 = jnp.zeros((bp, CONV_W - 1, CONV_DIM), xp.dtype)
        h0 = jnp.zeros((bp, SSM_HEADS, SSM_HEAD_DIM, SSM_STATE), xp.dtype)
        xp, conv_p, ssm_p, k_p, v_p = _group_step(xp, conv0, h0, None, None, mk_p, mv_p, lw)
        xp = _moe(xp.reshape(bp * lp, d), *moe_w, norm_final, last, _tile(bp * lp, 1024)).reshape(bp, lp, d)
        outs["skp"].append(k_p); outs["svp"].append(v_p); outs["ssp"].append(ssm_p); outs["scp"].append(conv_p)
        outs["mkp"].append(mk_p.reshape(bp, m, MEM_HEADS, MEM_HEAD_DIM))
        outs["mvp"].append(mv_p.reshape(bp, m, MEM_HEADS, MEM_HEAD_DIM))

        past = cache_sb_k.shape[2]
        ms = cache_mem_k.shape[2]
        xs, conv_s, ssm_s, k_s, v_s = _group_step(
            xs, state_conv[l], state_ssm[l], cache_sb_k[l].reshape(bs, past, SB_INNER),
            cache_sb_v[l].reshape(bs, past, SB_INNER), cache_mem_k[l].reshape(bs, ms, d),
            cache_mem_v[l].reshape(bs, ms, d), lw)
        xs = _moe(xs.reshape(bs * ls, d), *moe_w, norm_final, last, _tile(bs * ls, 1024)).reshape(bs, ls, d)
        outs["sks"].append(k_s); outs["svs"].append(v_s); outs["sss"].append(ssm_s); outs["scs"].append(conv_s)

    st = lambda name: jnp.stack(outs[name])
    return (xp, xs, st("skp"), st("svp"), st("ssp"), st("scp"), st("mkp"), st("mvp"),
            st("sks"), st("svs"), st("sss"), st("scs"))
```

```python
import functools

import jax
import jax.numpy as jnp
from jax import lax
from jax.experimental import pallas as pl
from jax.experimental.pallas import tpu as pltpu

F32 = jnp.float32
BF16 = jnp.bfloat16
EPS = 1e-6

D_MODEL = 1024
SSM_HEADS = 16
SSM_HEAD_DIM = 64
SSM_INNER = SSM_HEADS * SSM_HEAD_DIM
SSM_GROUPS = 2
SSM_STATE = 128
CONV_W = 4
CONV_DIM = SSM_INNER + 2 * SSM_GROUPS * SSM_STATE
SB_HEADS = 8
SB_HEAD_DIM = 64
SB_INNER = SB_HEADS * SB_HEAD_DIM
MEM_HEADS = 4
MEM_HEAD_DIM = D_MODEL // MEM_HEADS
MOE_GROUPS = 4
EXPERTS_PER_GROUP = 4
N_EXPERTS = MOE_GROUPS * EXPERTS_PER_GROUP
IN_SIZES = (SSM_INNER, CONV_DIM, SSM_HEADS, SB_INNER, SB_INNER, SB_INNER, D_MODEL, D_MODEL)

LANES_V7X = 128
SUBLANES_V7X = 8
VMEM_BYTES_V7X = 64 * 1024 * 1024
COMPILER_TEMP_BYTES = 20 * 1024 * 1024
VMEM_CAP_BYTES = 60000 * 1024

NT_DIMS = (((1,), (1,)), ((), ()))

SB_CHUNK = 512
SB_SUB = 256
SB_STACK_ROWS = 1024

MOE_TILE = 512
PERMUTE_ROWS = 512


def _vmem_limit(block_bytes, scratch_bytes=0):
    need = 2 * block_bytes + scratch_bytes + COMPILER_TEMP_BYTES
    return int(min(need, VMEM_CAP_BYTES))


def _nbytes(shape, dtype):
    n = 1
    for s in shape:
        n *= s
    return n * jnp.dtype(dtype).itemsize


def _sigmoid(x):
    return 1.0 / (1.0 + jnp.exp(-x))


def _softplus(x):
    return jnp.maximum(x, 0.0) + jnp.log1p(jnp.exp(-jnp.abs(x)))


def _rms_scale(x):
    return lax.rsqrt(jnp.mean(x * x, axis=-1, keepdims=True) + EPS)


def _norm_proj_kernel(x_ref, g_ref, *refs, out_plan):
    n_w = len(out_plan)
    w_refs = refs[:n_w]
    o_refs = refs[n_w:]
    x = x_ref[...]
    h = ((x * _rms_scale(x)) * g_ref[...]).astype(BF16)
    oi = 0
    for w_ref, dtypes in zip(w_refs, out_plan):
        y = jnp.dot(h, w_ref[...], preferred_element_type=F32)
        for dt in dtypes:
            o_refs[oi][...] = y.astype(dt)
            oi += 1


def _norm_proj(x2d, gain, weights, out_plan, tm, name):
    t, d = x2d.shape
    assert t % tm == 0
    in_specs = [pl.BlockSpec((tm, d), lambda i: (i, 0)), pl.BlockSpec((1, d), lambda i: (0, 0))]
    block_bytes = _nbytes((tm, d), F32) + _nbytes((1, d), F32)
    for w in weights:
        in_specs.append(pl.BlockSpec(w.shape, lambda i: (0, 0)))
        block_bytes += _nbytes(w.shape, w.dtype)
    out_shapes, out_specs = [], []
    for w, dtypes in zip(weights, out_plan):
        n = w.shape[1]
        for dt in dtypes:
            out_shapes.append(jax.ShapeDtypeStruct((t, n), dt))
            out_specs.append(pl.BlockSpec((tm, n), lambda i: (i, 0)))
            block_bytes += _nbytes((tm, max(n, LANES_V7X)), dt)
    return pl.pallas_call(
        functools.partial(_norm_proj_kernel, out_plan=tuple(out_plan)),
        grid=(t // tm,),
        in_specs=in_specs,
        out_specs=out_specs,
        out_shape=out_shapes,
        compiler_params=pltpu.CompilerParams(
            dimension_semantics=("parallel",), vmem_limit_bytes=_vmem_limit(block_bytes)),
        name=name,
    )(x2d, gain.reshape(1, d), *weights)


def _ssd_kernel(xbc_ref, dt_ref, z_ref, ctx_ref, h0_ref, convw_ref, convb_ref, dtb_ref, alog_ref,
                dskip_ref, norm_ref, yn_ref, ht_ref, ext_ref, y_ref, *, c):
    j = pl.program_id(1)
    halo = SUBLANES_V7X
    n = SSM_STATE
    pair_w = 2 * SSM_HEAD_DIM

    @pl.when(j == 0)
    def _():
        ext_ref[0:halo, :] = ctx_ref[0]
        ht_ref[0] = h0_ref[0]

    raw = xbc_ref[0]
    ext_ref[halo:halo + c, :] = raw
    conv = convb_ref[...]
    for tap in range(CONV_W - 1):
        off = halo - (CONV_W - 1) + tap
        conv = conv + ext_ref[off:off + c, :] * convw_ref[tap:tap + 1, :]
    conv = conv + raw * convw_ref[CONV_W - 1:CONV_W, :]
    ext_ref[0:halo, :] = ext_ref[c:c + halo, :]
    act = conv * _sigmoid(conv)

    dt = _softplus(dt_ref[0] + dtb_ref[...])
    a = -jnp.exp(alog_ref[...])
    row = lax.broadcasted_iota(jnp.int32, (c, c), 0)
    col = lax.broadcasted_iota(jnp.int32, (c, c), 1)
    tril = row >= col
    cum = jnp.dot(tril.astype(F32), dt * a, precision=lax.Precision.HIGHEST,
                  preferred_element_type=F32)
    hrow = lax.broadcasted_iota(jnp.int32, (SSM_HEADS, SSM_HEADS), 0)
    hcol = lax.broadcasted_iota(jnp.int32, (SSM_HEADS, SSM_HEADS), 1)
    eye_h = (hrow == hcol).astype(F32)
    cum_t = lax.dot_general(eye_h, cum, NT_DIMS, precision=lax.Precision.HIGHEST,
                            preferred_element_type=F32)
    cum_last = cum[c - 1:c, :]
    exp_cum = jnp.exp(cum)
    dt_end = dt * jnp.exp(cum_last - cum)
    chunk_decay = jnp.exp(cum_last)

    nrow = lax.broadcasted_iota(jnp.int32, (n, n), 0)
    ncol = lax.broadcasted_iota(jnp.int32, (n, n), 1)
    eye_n = (nrow == ncol).astype(BF16)
    lane_c = lax.broadcasted_iota(jnp.int32, (c, pair_w), 1) < SSM_HEAD_DIM
    lane_n = lax.broadcasted_iota(jnp.int32, (n, pair_w), 1) < SSM_HEAD_DIM
    lane_1 = lax.broadcasted_iota(jnp.int32, (1, pair_w), 1) < SSM_HEAD_DIM

    heads_per_group = SSM_HEADS // SSM_GROUPS
    for g in range(SSM_GROUPS):
        b_lo = SSM_INNER + g * n
        c_lo = SSM_INNER + SSM_GROUPS * n + g * n
        bm = act[:, b_lo:b_lo + n].astype(BF16)
        cm = act[:, c_lo:c_lo + n]
        cb = lax.dot_general(cm.astype(BF16), bm, NT_DIMS, preferred_element_type=F32)
        bm_t = lax.dot_general(eye_n, bm, NT_DIMS, preferred_element_type=F32).astype(BF16)
        for pair in range(heads_per_group // 2):
            ha = g * heads_per_group + 2 * pair
            hb = ha + 1
            lo = ha * SSM_HEAD_DIM
            x_pair = act[:, lo:lo + pair_w]
            st = ht_ref[0, :, lo:lo + pair_w]
            dt_pair = jnp.where(lane_c, dt[:, ha:ha + 1], dt[:, hb:hb + 1])
            dte_pair = jnp.where(lane_c, dt_end[:, ha:ha + 1], dt_end[:, hb:hb + 1])
            rhs = jnp.concatenate([(x_pair * dt_pair).astype(BF16), st.astype(BF16)], axis=0)
            ys = []
            for h in (ha, hb):
                seg = cum[:, h:h + 1] - cum_t[h:h + 1, :]
                decay = jnp.exp(jnp.where(tril, seg, -jnp.inf))
                lhs = jnp.concatenate(
                    [(cb * decay).astype(BF16), (cm * exp_cum[:, h:h + 1]).astype(BF16)], axis=1)
                ys.append(jnp.dot(lhs, rhs, preferred_element_type=F32))
            y_ref[:, lo:lo + pair_w] = jnp.where(lane_c, ys[0], ys[1])
            upd = jnp.dot(bm_t, (x_pair * dte_pair).astype(BF16), preferred_element_type=F32)
            cd_pair = jnp.where(lane_1, chunk_decay[:, ha:ha + 1], chunk_decay[:, hb:hb + 1])
            ht_ref[0, :, lo:lo + pair_w] = st * cd_pair + upd

    y = y_ref[...] + dskip_ref[...] * act[:, :SSM_INNER]
    zz = z_ref[0]
    gated = y * (zz * _sigmoid(zz))
    yn_ref[0] = ((gated * _rms_scale(gated)) * norm_ref[...]).astype(yn_ref.dtype)


def _ssd(xbc, dt, z, ctx8, h0t, conv_w, conv_b, dt_bias, a_log, d_skip, ssm_norm, c):
    b, l, _ = xbc.shape
    assert l % c == 0 and c % SUBLANES_V7X == 0
    n = SSM_STATE
    per_b = lambda shape: pl.BlockSpec(shape, lambda i, j: (i, 0, 0))
    chunk = lambda w: pl.BlockSpec((1, c, w), lambda i, j: (i, j, 0))
    full = lambda shape: pl.BlockSpec(shape, lambda i, j: (0, 0))
    block_bytes = (_nbytes((c, CONV_DIM), F32) + _nbytes((c, LANES_V7X), F32) + _nbytes((c, SSM_INNER), F32)
                   + _nbytes((8, CONV_DIM), F32) + 2 * _nbytes((n, SSM_INNER), F32)
                   + _nbytes((c, SSM_INNER), BF16) + 8 * _nbytes((8, CONV_DIM), F32))
    scratch_bytes = _nbytes((c + 8, CONV_DIM), F32) + _nbytes((c, SSM_INNER), F32)
    return pl.pallas_call(
        functools.partial(_ssd_kernel, c=c),
        grid=(b, l // c),
        in_specs=[chunk(CONV_DIM), chunk(SSM_HEADS), chunk(SSM_INNER), per_b((1, 8, CONV_DIM)),
                  per_b((1, n, SSM_INNER)), full((CONV_W, CONV_DIM)), full((1, CONV_DIM)),
                  full((1, SSM_HEADS)), full((1, SSM_HEADS)), full((1, SSM_INNER)), full((1, SSM_INNER))],
        out_specs=[chunk(SSM_INNER), per_b((1, n, SSM_INNER))],
        out_shape=[jax.ShapeDtypeStruct((b, l, SSM_INNER), BF16),
                   jax.ShapeDtypeStruct((b, n, SSM_INNER), F32)],
        scratch_shapes=[pltpu.VMEM((c + 8, CONV_DIM), F32), pltpu.VMEM((c, SSM_INNER), F32)],
        compiler_params=pltpu.CompilerParams(
            dimension_semantics=("parallel", "arbitrary"),
            vmem_limit_bytes=_vmem_limit(block_bytes, scratch_bytes)),
        name="ssd",
    )(xbc, dt, z, ctx8, h0t, conv_w, conv_b.reshape(1, -1), dt_bias.reshape(1, -1), a_log.reshape(1, -1),
      jnp.repeat(d_skip, SSM_HEAD_DIM).reshape(1, -1), ssm_norm.reshape(1, -1))


def _sb_attn_kernel(*refs, tq, tkp, n_past, hps, scale):
    if n_past:
        q_ref, kn_ref, vn_ref, kp_ref, vp_ref, o_ref, acc_ref, run_ref = refs
    else:
        q_ref, kn_ref, vn_ref, o_ref, acc_ref, run_ref = refs
    qi = pl.program_id(2)
    width = hps * SB_HEAD_DIM
    m = hps * tq
    q = (q_ref[0].astype(F32) * scale).astype(BF16)
    head_of_lane = lax.broadcasted_iota(jnp.int32, (tq, width), 1) // SB_HEAD_DIM
    zero = jnp.zeros_like(q)
    q_stack = jnp.concatenate([jnp.where(head_of_lane == h, q, zero) for h in range(hps)], axis=0)

    def suffix_ones(t):
        r = lax.broadcasted_iota(jnp.int32, (t, t), 0)
        s = lax.broadcasted_iota(jnp.int32, (t, t), 1)
        return jnp.where(r > s, -1.0, 0.0).astype(BF16)

    acc_ref[...] = jnp.zeros_like(acc_ref)
    run_ref[...] = jnp.zeros_like(run_ref)
    sign_bit = jnp.int32(-2 ** 31)

    def visit(k, v, mask):
        ck = k.shape[0]
        sub = min(ck, SB_SUB)
        ones = suffix_ones(sub)
        zs = lax.dot_general(q_stack, k, NT_DIMS, preferred_element_type=F32)
        neg_abs = lax.bitcast_convert_type(lax.bitcast_convert_type(zs, jnp.int32) | sign_bit, F32)
        drop = jnp.maximum(zs, 0.0) + jnp.log(1.0 + jnp.exp(neg_abs))
        if mask is not None:
            drop = jnp.where(mask, drop, 0.0)
        drop_b = drop.astype(BF16)
        log_beta = zs - drop
        run = run_ref[...]
        ws = []
        for j in reversed(range(ck // sub)):
            sl = slice(j * sub, (j + 1) * sub)
            suffix = jnp.dot(drop_b[:, sl], ones, preferred_element_type=F32)
            w = jnp.exp(log_beta[:, sl] + suffix + run)
            if mask is not None:
                w = jnp.where(mask[:, sl], w, 0.0)
            ws.insert(0, w.astype(BF16))
            first = drop_b[:, j * sub:j * sub + 1].astype(F32)
            run = run + (suffix[:, 0:1] - first)
        w_all = ws[0] if len(ws) == 1 else jnp.concatenate(ws, axis=1)
        acc_ref[...] += jnp.dot(w_all, v, preferred_element_type=F32)
        run_ref[...] = run

    r = lax.rem(lax.broadcasted_iota(jnp.int32, (m, tq), 0), tq)
    s = lax.broadcasted_iota(jnp.int32, (m, tq), 1)
    d0 = pl.multiple_of(qi * tq, tq)
    visit(kn_ref[0, pl.ds(d0, tq), :], vn_ref[0, pl.ds(d0, tq), :], s < r)

    def new_body(i, carry):
        off = pl.multiple_of((qi - 1 - i) * tq, tq)
        visit(kn_ref[0, pl.ds(off, tq), :], vn_ref[0, pl.ds(off, tq), :], None)
        return carry

    lax.fori_loop(0, qi, new_body, 0)

    if n_past:
        def past_body(i, carry):
            off = pl.multiple_of((n_past - 1 - i) * tkp, tkp)
            visit(kp_ref[0, pl.ds(off, tkp), :].astype(BF16), vp_ref[0, pl.ds(off, tkp), :].astype(BF16), None)
            return carry

        lax.fori_loop(0, n_past, past_body, 0)

    out = acc_ref[0:tq, :]
    for h in range(1, hps):
        out = jnp.where(head_of_lane == h, acc_ref[h * tq:(h + 1) * tq, :], out)
    o_ref[0] = out.astype(o_ref.dtype)


def _sb_attn(q, k_new, v_new, k_past, v_past, tq, tkp, hps):
    b, l, _ = q.shape
    assert l % tq == 0 and SB_HEADS % hps == 0
    width = hps * SB_HEAD_DIM
    m = hps * tq
    n_past = 0
    tile = pl.BlockSpec((1, tq, width), lambda i, hq, qi: (i, qi, hq))
    whole = lambda length: pl.BlockSpec((1, length, width), lambda i, hq, qi: (i, 0, hq))
    in_specs = [tile, whole(l), whole(l)]
    args = [q, k_new, v_new]
    block_bytes = 2 * _nbytes((tq, width), BF16) + 2 * _nbytes((l, width), BF16)
    if k_past is not None:
        p = k_past.shape[1]
        assert p % tkp == 0
        n_past = p // tkp
        in_specs += [whole(p), whole(p)]
        args += [k_past, v_past]
        block_bytes += 2 * _nbytes((p, width), F32)
    scratch_bytes = _nbytes((m, width), F32) + _nbytes((m, LANES_V7X), F32)
    return pl.pallas_call(
        functools.partial(_sb_attn_kernel, tq=tq, tkp=tkp, n_past=n_past, hps=hps, scale=SB_HEAD_DIM ** -0.5),
        grid=(b, SB_HEADS // hps, l // tq),
        in_specs=in_specs,
        out_specs=tile,
        out_shape=jax.ShapeDtypeStruct((b, l, SB_INNER), BF16),
        scratch_shapes=[pltpu.VMEM((m, width), F32), pltpu.VMEM((m, 1), F32)],
        compiler_params=pltpu.CompilerParams(
            dimension_semantics=("parallel", "parallel", "arbitrary"),
            vmem_limit_bytes=_vmem_limit(block_bytes, scratch_bytes)),
        name="sb_attn",
    )(*args)


def _merge_mem_kernel(x_ref, yn_ref, ysb_ref, gs_ref, gb_ref, mk_ref, mv_ref, wssm_ref, wsb_ref, wout_ref,
                      nq_ref, wq_ref, wo_ref, o_ref, *, scale):
    y_ssm = jnp.dot(yn_ref[0], wssm_ref[...], preferred_element_type=F32)
    y_sb = jnp.dot(ysb_ref[0], wsb_ref[...], preferred_element_type=F32)
    merged = _sigmoid(gs_ref[0]) * y_ssm + _sigmoid(gb_ref[0]) * y_sb
    x1 = x_ref[0] + jnp.dot(merged.astype(BF16), wout_ref[...], preferred_element_type=F32)
    hq = ((x1 * _rms_scale(x1)) * nq_ref[...]).astype(BF16)
    q = jnp.dot(hq, wq_ref[...], preferred_element_type=F32)
    outs = []
    for h in range(MEM_HEADS):
        sl = slice(h * MEM_HEAD_DIM, (h + 1) * MEM_HEAD_DIM)
        mk = mk_ref[0, :, sl].astype(BF16)
        mv = mv_ref[0, :, sl].astype(BF16)
        s = lax.dot_general(q[:, sl].astype(BF16), mk, NT_DIMS, preferred_element_type=F32) * scale
        e = jnp.exp(s - jnp.max(s, axis=-1, keepdims=True))
        p = e / jnp.sum(e, axis=-1, keepdims=True)
        outs.append(jnp.dot(p.astype(BF16), mv, preferred_element_type=F32))
    o = jnp.concatenate(outs, axis=1)
    o_ref[0] = x1 + jnp.dot(o.astype(BF16), wo_ref[...], preferred_element_type=F32)


def _merge_mem(x, yn, ysb, gs, gb, mk, mv, w_ssm, w_sb, w_out, norm_q, w_q, w_o, tm):
    b, l, d = x.shape
    m = mk.shape[1]
    assert l % tm == 0
    tok = lambda w: pl.BlockSpec((1, tm, w), lambda i, j: (i, j, 0))
    per_b = pl.BlockSpec((1, m, d), lambda i, j: (i, 0, 0))
    full = lambda a: pl.BlockSpec(a.shape, lambda i, j: (0, 0))
    nq = norm_q.reshape(1, d)
    weights = (w_ssm, w_sb, w_out, nq, w_q, w_o)
    block_bytes = (4 * _nbytes((tm, d), F32) + _nbytes((tm, SSM_INNER + SB_INNER), BF16)
                   + 2 * _nbytes((m, d), F32) + sum(_nbytes(w.shape, w.dtype) for w in weights))
    return pl.pallas_call(
        functools.partial(_merge_mem_kernel, scale=MEM_HEAD_DIM ** -0.5),
        grid=(b, l // tm),
        in_specs=[tok(d), tok(SSM_INNER), tok(SB_INNER), tok(d), tok(d), per_b, per_b] + [full(w) for w in weights],
        out_specs=tok(d),
        out_shape=jax.ShapeDtypeStruct((b, l, d), F32),
        compiler_params=pltpu.CompilerParams(
            dimension_semantics=("parallel", "parallel"), vmem_limit_bytes=_vmem_limit(block_bytes)),
        name="merge_mem",
    )(x, yn, ysb, gs, gb, mk, mv, *weights)


def _route(logits):
    lane = lax.broadcasted_iota(jnp.int32, logits.shape, 1)
    big = jnp.int32(LANES_V7X)
    neg = -jnp.inf
    is_group = lane < MOE_GROUPS
    gl = jnp.where(is_group, logits, neg)
    ge = jnp.where(is_group, jnp.exp(gl - jnp.max(gl, axis=-1, keepdims=True)), 0.0)
    pg = ge / jnp.sum(ge, axis=-1, keepdims=True)
    p_sel = jnp.max(pg, axis=-1, keepdims=True)
    g_sel = jnp.min(jnp.where(is_group & (pg == p_sel), lane, big), axis=-1, keepdims=True)
    e_idx = lane - MOE_GROUPS
    in_group = (e_idx >= 0) & (e_idx < N_EXPERTS) & ((e_idx // EXPERTS_PER_GROUP) == g_sel)
    le = jnp.where(in_group, logits, neg)
    v1 = jnp.max(le, axis=-1, keepdims=True)
    i1 = jnp.min(jnp.where(in_group & (le == v1), lane, big), axis=-1, keepdims=True)
    rest = in_group & (lane != i1)
    le2 = jnp.where(rest, logits, neg)
    v2 = jnp.max(le2, axis=-1, keepdims=True)
    i2 = jnp.min(jnp.where(rest & (le2 == v2), lane, big), axis=-1, keepdims=True)
    e2 = jnp.exp(v2 - v1)
    den = 1.0 + e2
    dense_w = (jnp.where(lane == i1, (1.0 / den) * p_sel, 0.0)
               + jnp.where(lane == i2, (e2 / den) * p_sel, 0.0))
    return dense_w, g_sel


def _moe_hidden(x_ref, nf_ref):
    x = x_ref[...]
    return ((x * _rms_scale(x)) * nf_ref[...]).astype(BF16)


def _moe_rank_kernel(x_ref, nf_ref, wr_ref, info_ref, cnt_ref, carry_ref):
    i = pl.program_id(0)

    @pl.when(i == 0)
    def _():
        carry_ref[...] = jnp.zeros_like(carry_ref)

    hb = _moe_hidden(x_ref, nf_ref)
    _, g_sel = _route(jnp.dot(hb, wr_ref[...], preferred_element_type=F32))
    tm = hb.shape[0]
    lane = lax.broadcasted_iota(jnp.int32, (tm, LANES_V7X), 1)
    onehot = lane == g_sel
    r = lax.broadcasted_iota(jnp.int32, (tm, tm), 0)
    c = lax.broadcasted_iota(jnp.int32, (tm, tm), 1)
    before = jnp.dot((c < r).astype(BF16), onehot.astype(BF16), preferred_element_type=F32) + carry_ref[...]
    rank = jnp.sum(jnp.where(onehot, before, 0.0), axis=-1, keepdims=True)
    info = jnp.where(lane == 0, g_sel.astype(F32), jnp.where(lane == 1, rank, 0.0))
    info_ref[...] = info.astype(jnp.int32)
    carry_ref[...] += jnp.sum(onehot.astype(F32), axis=0, keepdims=True)
    cnt_ref[...] = carry_ref[...].astype(jnp.int32)


def _moe_experts_kernel(tile_ref, exp_ref, first_ref, last_ref, act_ref, x_ref, nf_ref, wr_ref, wg_ref, wu_ref,
                        wd_ref, nfin_ref, y_ref, h_ref, dw_ref, acc_ref, *, final_norm):
    w = pl.program_id(0)

    @pl.when(first_ref[w] == 1)
    def _():
        hb = _moe_hidden(x_ref, nf_ref)
        h_ref[...] = hb
        dw_ref[...] = _route(jnp.dot(hb, wr_ref[...], preferred_element_type=F32))[0]
        acc_ref[...] = jnp.zeros_like(acc_ref)

    @pl.when(act_ref[w] == 1)
    def _():
        hb = h_ref[...]
        gate = jnp.dot(hb, wg_ref[0], preferred_element_type=F32)
        up = jnp.dot(hb, wu_ref[0], preferred_element_type=F32)
        he = (gate * _sigmoid(gate)) * up
        down = jnp.dot(he.astype(BF16), wd_ref[0], preferred_element_type=F32)
        lane = lax.broadcasted_iota(jnp.int32, dw_ref.shape, 1)
        dcol = jnp.sum(jnp.where(lane == exp_ref[w] + MOE_GROUPS, dw_ref[...], 0.0), axis=-1, keepdims=True)
        acc_ref[...] += dcol * down

    @pl.when(last_ref[w] == 1)
    def _():
        xo = x_ref[...] + acc_ref[...]
        y_ref[...] = (xo * _rms_scale(xo)) * nfin_ref[...] if final_norm else xo


def _permute_rows_kernel(idx_ref, src_ref, dst_ref, sem, *, rows, scatter):
    s = pl.program_id(0)
    slot = s % 2

    def row_copy(j, k, sl):
        src_row, dst_row = (j, k) if scatter else (k, j)
        return pltpu.make_async_copy(src_ref.at[pl.ds(src_row, 1)], dst_ref.at[pl.ds(dst_row, 1)], sem.at[sl])

    def start(r, carry):
        row_copy(s * rows + r, idx_ref[0, 0, r], slot).start()
        return carry

    lax.fori_loop(0, rows, start, 0, unroll=8)

    def wait_all(sl):
        def wait(r, carry):
            row_copy(0, 0, sl).wait()
            return carry
        lax.fori_loop(0, rows, wait, 0, unroll=8)

    @pl.when(s > 0)
    def _():
        wait_all(1 - slot)

    @pl.when(s == pl.num_programs(0) - 1)
    def _():
        wait_all(slot)


def _permute_rows(src, idx, scatter, name):
    t, d = src.shape
    rows = _tile(t, PERMUTE_ROWS)
    return pl.pallas_call(
        functools.partial(_permute_rows_kernel, rows=rows, scatter=scatter),
        grid=(t // rows,),
        in_specs=[pl.BlockSpec((1, 1, rows), lambda s: (s, 0, 0), memory_space=pltpu.SMEM),
                  pl.BlockSpec(memory_space=pl.ANY)],
        out_specs=pl.BlockSpec(memory_space=pl.ANY),
        out_shape=jax.ShapeDtypeStruct((t, d), src.dtype),
        scratch_shapes=[pltpu.SemaphoreType.DMA((2,))],
        compiler_params=pltpu.CompilerParams(dimension_semantics=("arbitrary",)),
        name=name,
    )(idx.reshape(t // rows, 1, rows), src)


def _moe_plan(gid, rank, counts, n_tiles, tm):
    i32 = jnp.int32
    ends = jnp.cumsum(counts).astype(i32)
    pos = (ends - counts)[gid] + rank
    first_row = jnp.arange(n_tiles, dtype=i32) * tm
    groups_before = lambda row: jnp.sum((ends[None, :] <= row[:, None]).astype(i32), axis=1)
    g_lo, g_hi = groups_before(first_row), groups_before(first_row + (tm - 1))
    n_items = EXPERTS_PER_GROUP * (g_hi - g_lo + 1)
    item_end = jnp.cumsum(n_items).astype(i32)
    n_work = EXPERTS_PER_GROUP * (n_tiles + MOE_GROUPS - 1)
    w = jnp.arange(n_work, dtype=i32)
    tile = jnp.minimum(jnp.sum((item_end[None, :] <= w[:, None]).astype(i32), axis=1), n_tiles - 1)
    local = w - (item_end - n_items)[tile]
    active = w < item_end[-1]
    expert = jnp.where(active, EXPERTS_PER_GROUP * g_lo[tile] + local,
                       EXPERTS_PER_GROUP * g_hi[-1] + EXPERTS_PER_GROUP - 1)
    first = active & (local == 0)
    last = active & (local == n_items[tile] - 1)
    return pos.astype(i32), (tile, expert.astype(i32), first.astype(i32), last.astype(i32), active.astype(i32))


def _moe(x2d, norm_ffn, w_router, w_gate, w_up, w_down, norm_final, final_norm, tm):
    t, d = x2d.shape
    assert t % tm == 0 and t < 2 ** 24
    n_tiles = t // tm
    n_e, _, ff = w_gate.shape
    nf, nfin = norm_ffn.reshape(1, d), norm_final.reshape(1, d)
    const = lambda a: pl.BlockSpec(a.shape, lambda i, *_: (0, 0))

    rank_bytes = _nbytes((tm, d), F32) + _nbytes(w_router.shape, BF16) + _nbytes((tm, LANES_V7X), F32)
    info, cnt = pl.pallas_call(
        _moe_rank_kernel,
        grid=(n_tiles,),
        in_specs=[pl.BlockSpec((tm, d), lambda i: (i, 0)), const(nf), const(w_router)],
        out_specs=[pl.BlockSpec((tm, LANES_V7X), lambda i: (i, 0)), pl.BlockSpec((1, LANES_V7X), lambda i: (0, 0))],
        out_shape=[jax.ShapeDtypeStruct((t, LANES_V7X), jnp.int32), jax.ShapeDtypeStruct((1, LANES_V7X), jnp.int32)],
        scratch_shapes=[pltpu.VMEM((1, LANES_V7X), F32)],
        compiler_params=pltpu.CompilerParams(
            dimension_semantics=("arbitrary",), vmem_limit_bytes=_vmem_limit(rank_bytes)),
        name="moe_rank",
    )(x2d, nf, w_router)
    pos, plan = _moe_plan(info[:, 0], info[:, 1], cnt[0, :MOE_GROUPS], n_tiles, tm)

    xs = _permute_rows(x2d, pos, True, "moe_sort")
    by_tile = lambda wi, tile, *_: (tile[wi], 0)
    by_expert = lambda wi, tile, exp, *_: (exp[wi], 0, 0)
    block_bytes = (2 * _nbytes((tm, d), F32) + 2 * _nbytes((1, d), F32) + _nbytes(w_router.shape, BF16)
                   + 3 * _nbytes((d, ff), BF16))
    scratch_bytes = _nbytes((tm, d), BF16) + _nbytes((tm, LANES_V7X), F32) + _nbytes((tm, d), F32)
    ys = pl.pallas_call(
        functools.partial(_moe_experts_kernel, final_norm=final_norm),
        grid_spec=pltpu.PrefetchScalarGridSpec(
            num_scalar_prefetch=len(plan),
            grid=(plan[0].shape[0],),
            in_specs=[pl.BlockSpec((tm, d), by_tile), const(nf), const(w_router),
                      pl.BlockSpec((1, d, ff), by_expert), pl.BlockSpec((1, d, ff), by_expert),
                      pl.BlockSpec((1, ff, d), by_expert), const(nfin)],
            out_specs=pl.BlockSpec((tm, d), by_tile),
            scratch_shapes=[pltpu.VMEM((tm, d), BF16), pltpu.VMEM((tm, LANES_V7X), F32), pltpu.VMEM((tm, d), F32)]),
        out_shape=jax.ShapeDtypeStruct((t, d), F32),
        compiler_params=pltpu.CompilerParams(
            dimension_semantics=("arbitrary",), vmem_limit_bytes=_vmem_limit(block_bytes, scratch_bytes)),
        name="moe_experts",
    )(*plan, xs, nf, w_router, w_gate, w_up, w_down, nfin)
    return _permute_rows(ys, pos, False, "moe_unsort")


def _tile(n, pref):
    t = min(n, pref)
    while n % t:
        t //= 2
    return t


def _split_w_in(w_in):
    offs = [0]
    for s in IN_SIZES:
        offs.append(offs[-1] + s)
    return [w_in[:, offs[i]:offs[i + 1]].astype(BF16) for i in range(len(IN_SIZES))]


def _group_step(x, conv_ctx, ssm_h0, past_k, past_v, mem_k, mem_v, lw):
    b, l, d = x.shape
    t = b * l
    x2d = x.reshape(t, d)
    w_z, w_xbc, w_dt, w_q, w_k, w_v, w_gs, w_gb = lw["w_in_parts"]
    z, xbc, dt, q, k, kb, v, vb, gs, gb = _norm_proj(
        x2d, lw["norm_mix"], [w_z, w_xbc, w_dt, w_q, w_k, w_v, w_gs, w_gb],
        [(F32,), (F32,), (F32,), (BF16,), (F32, BF16), (F32, BF16), (F32,), (F32,)],
        _tile(t, 256), "in_proj")
    r3 = lambda a: a.reshape(b, l, a.shape[-1])

    h0t = ssm_h0.transpose(0, 3, 1, 2).reshape(b, SSM_STATE, SSM_INNER)
    ctx8 = jnp.pad(conv_ctx, ((0, 0), (SUBLANES_V7X - (CONV_W - 1), 0), (0, 0)))
    yn, ht = _ssd(r3(xbc), r3(dt), r3(z), ctx8, h0t, lw["conv_w"], lw["conv_b"], lw["dt_bias"], lw["a_log"],
                  lw["d_skip"], lw["ssm_norm"], _tile(l, 128))
    ssm_new = ht.reshape(b, SSM_STATE, SSM_HEADS, SSM_HEAD_DIM).transpose(0, 2, 3, 1)
    assert l >= CONV_W - 1
    conv_new = r3(xbc)[:, l - (CONV_W - 1):, :]

    tq = _tile(l, SB_CHUNK)
    hps = min(SB_HEADS, SB_STACK_ROWS // tq)
    ysb = _sb_attn(r3(q), r3(kb), r3(vb), past_k, past_v, tq, SB_CHUNK, hps)

    x_mem = _merge_mem(x, yn, ysb, r3(gs), r3(gb), mem_k, mem_v, lw["w_ssm_br"], lw["w_sb_br"], lw["w_out"],
                       lw["norm_mem_q"], lw["w_mem_q"], lw["w_mem_o"], _tile(l, 256))
    kh = k.reshape(b, l, SB_HEADS, SB_HEAD_DIM)
    vh = v.reshape(b, l, SB_HEADS, SB_HEAD_DIM)
    return x_mem, conv_new, ssm_new, kh, vh


def kernel(x_prompt, x_sample, cache_sb_k, cache_sb_v, state_ssm, state_conv, cache_mem_k, cache_mem_v, mem_prompt, norm_mix, w_in, conv_w, conv_b, dt_bias, a_log, d_skip, ssm_norm, w_ssm_br, w_sb_br, w_out, norm_mem_q, norm_mem_kv, w_mem_q, w_mem_kv, w_mem_o, norm_ffn, w_group_router, w_expert_router, w_gate_e, w_up_e, w_down_e, norm_final):
    depth = norm_mix.shape[0]
    bp, lp, d = x_prompt.shape
    bs, ls, _ = x_sample.shape
    xp, xs = x_prompt, x_sample
    outs = {name: [] for name in ("skp", "svp", "ssp", "scp", "mkp", "mvp", "sks", "svs", "sss", "scs")}
    for l in range(depth):
        router = jnp.concatenate([w_group_router[l], w_expert_router[l]], axis=1)
        router = jnp.pad(router, ((0, 0), (0, LANES_V7X - router.shape[1]))).astype(BF16)
        lw = dict(
            norm_mix=norm_mix[l], w_in_parts=_split_w_in(w_in[l]), conv_w=conv_w[l], conv_b=conv_b[l],
            dt_bias=dt_bias[l], a_log=a_log[l], d_skip=d_skip[l], ssm_norm=ssm_norm[l],
            w_ssm_br=w_ssm_br[l].astype(BF16), w_sb_br=w_sb_br[l].astype(BF16), w_out=w_out[l].astype(BF16),
            norm_mem_q=norm_mem_q[l], w_mem_q=w_mem_q[l].astype(BF16), w_mem_o=w_mem_o[l].astype(BF16))
        moe_w = (norm_ffn[l], router, w_gate_e[l].astype(BF16), w_up_e[l].astype(BF16), w_down_e[l].astype(BF16))
        last = l == depth - 1

        m = mem_prompt.shape[1]
        w_kv = w_mem_kv[l].astype(BF16)
        mk_p, mv_p = _norm_proj(mem_prompt.reshape(bp * m, d), norm_mem_kv[l], [w_kv[:, :d], w_kv[:, d:]],
                                [(F32,), (F32,)], _tile(bp * m, 256), "mem_kv")
        mk_p, mv_p = mk_p.reshape(bp, m, d), mv_p.reshape(bp, m, d)
        conv0 = jnp.zeros((bp, CONV_W - 1, CONV_DIM), xp.dtype)
        h0 = jnp.zeros((bp, SSM_HEADS, SSM_HEAD_DIM, SSM_STATE), xp.dtype)
        xp, conv_p, ssm_p, k_p, v_p = _group_step(xp, conv0, h0, None, None, mk_p, mv_p, lw)
        xp = _moe(xp.reshape(bp * lp, d), *moe_w, norm_final, last, _tile(bp * lp, MOE_TILE)).reshape(bp, lp, d)
        outs["skp"].append(k_p); outs["svp"].append(v_p); outs["ssp"].append(ssm_p); outs["scp"].append(conv_p)
        outs["mkp"].append(mk_p.reshape(bp, m, MEM_HEADS, MEM_HEAD_DIM))
        outs["mvp"].append(mv_p.reshape(bp, m, MEM_HEADS, MEM_HEAD_DIM))

        past = cache_sb_k.shape[2]
        ms = cache_mem_k.shape[2]
        xs, conv_s, ssm_s, k_s, v_s = _group_step(
            xs, state_conv[l], state_ssm[l], cache_sb_k[l].reshape(bs, past, SB_INNER),
            cache_sb_v[l].reshape(bs, past, SB_INNER), cache_mem_k[l].reshape(bs, ms, d),
            cache_mem_v[l].reshape(bs, ms, d), lw)
        xs = _moe(xs.reshape(bs * ls, d), *moe_w, norm_final, last, _tile(bs * ls, MOE_TILE)).reshape(bs, ls, d)
        outs["sks"].append(k_s); outs["svs"].append(v_s); outs["sss"].append(ssm_s); outs["scs"].append(conv_s)

    st = lambda name: jnp.stack(outs[name])
    return (xp, xs, st("skp"), st("svp"), st("ssp"), st("scp"), st("mkp"), st("mvp"),
            st("sks"), st("svs"), st("sss"), st("scs"))
```

```python
import functools

import jax
import jax.numpy as jnp
from jax import lax
from jax.experimental import pallas as pl
from jax.experimental.pallas import tpu as pltpu

F32 = jnp.float32
BF16 = jnp.bfloat16
EPS = 1e-6

D_MODEL = 1024
SSM_HEADS = 16
SSM_HEAD_DIM = 64
SSM_INNER = SSM_HEADS * SSM_HEAD_DIM
SSM_GROUPS = 2
SSM_STATE = 128
CONV_W = 4
CONV_DIM = SSM_INNER + 2 * SSM_GROUPS * SSM_STATE
SB_HEADS = 8
SB_HEAD_DIM = 64
SB_INNER = SB_HEADS * SB_HEAD_DIM
MEM_HEADS = 4
MEM_HEAD_DIM = D_MODEL // MEM_HEADS
MOE_GROUPS = 4
EXPERTS_PER_GROUP = 4
N_EXPERTS = MOE_GROUPS * EXPERTS_PER_GROUP
IN_SIZES = (SSM_INNER, CONV_DIM, SSM_HEADS, SB_INNER, SB_INNER, SB_INNER, D_MODEL, D_MODEL)

LANES_V7X = 128
SUBLANES_V7X = 8
VMEM_BYTES_V7X = 64 * 1024 * 1024
COMPILER_TEMP_BYTES = 20 * 1024 * 1024
VMEM_CAP_BYTES = 60000 * 1024

NT_DIMS = (((1,), (1,)), ((), ()))

SB_CHUNK = 512
SB_SUB = 256
SB_STACK_ROWS = 1024

MOE_TILE = 512
PERMUTE_ROWS = 1024


def _vmem_limit(block_bytes, scratch_bytes=0):
    need = 2 * block_bytes + scratch_bytes + COMPILER_TEMP_BYTES
    return int(min(need, VMEM_CAP_BYTES))


def _nbytes(shape, dtype):
    n = 1
    for s in shape:
        n *= s
    return n * jnp.dtype(dtype).itemsize


def _sigmoid(x):
    return 1.0 / (1.0 + jnp.exp(-x))


def _softplus(x):
    return jnp.maximum(x, 0.0) + jnp.log1p(jnp.exp(-jnp.abs(x)))


def _rms_scale(x):
    return lax.rsqrt(jnp.mean(x * x, axis=-1, keepdims=True) + EPS)


def _norm_proj_kernel(x_ref, g_ref, *refs, out_plan):
    n_w = len(out_plan)
    w_refs = refs[:n_w]
    o_refs = refs[n_w:]
    x = x_ref[...]
    h = ((x * _rms_scale(x)) * g_ref[...]).astype(BF16)
    oi = 0
    for w_ref, dtypes in zip(w_refs, out_plan):
        y = jnp.dot(h, w_ref[...], preferred_element_type=F32)
        for dt in dtypes:
            o_refs[oi][...] = y.astype(dt)
            oi += 1


def _norm_proj(x2d, gain, weights, out_plan, tm, name):
    t, d = x2d.shape
    assert t % tm == 0
    in_specs = [pl.BlockSpec((tm, d), lambda i: (i, 0)), pl.BlockSpec((1, d), lambda i: (0, 0))]
    block_bytes = _nbytes((tm, d), F32) + _nbytes((1, d), F32)
    for w in weights:
        in_specs.append(pl.BlockSpec(w.shape, lambda i: (0, 0)))
        block_bytes += _nbytes(w.shape, w.dtype)
    out_shapes, out_specs = [], []
    for w, dtypes in zip(weights, out_plan):
        n = w.shape[1]
        for dt in dtypes:
            out_shapes.append(jax.ShapeDtypeStruct((t, n), dt))
            out_specs.append(pl.BlockSpec((tm, n), lambda i: (i, 0)))
            block_bytes += _nbytes((tm, max(n, LANES_V7X)), dt)
    return pl.pallas_call(
        functools.partial(_norm_proj_kernel, out_plan=tuple(out_plan)),
        grid=(t // tm,),
        in_specs=in_specs,
        out_specs=out_specs,
        out_shape=out_shapes,
        compiler_params=pltpu.CompilerParams(
            dimension_semantics=("parallel",), vmem_limit_bytes=_vmem_limit(block_bytes)),
        name=name,
    )(x2d, gain.reshape(1, d), *weights)


def _ssd_kernel(xbc_ref, dt_ref, z_ref, ctx_ref, h0_ref, convw_ref, convb_ref, dtb_ref, alog_ref,
                dskip_ref, norm_ref, yn_ref, ht_ref, ext_ref, y_ref, *, c):
    j = pl.program_id(1)
    halo = SUBLANES_V7X
    n = SSM_STATE
    pair_w = 2 * SSM_HEAD_DIM

    @pl.when(j == 0)
    def _():
        ext_ref[0:halo, :] = ctx_ref[0]
        ht_ref[0] = h0_ref[0]

    raw = xbc_ref[0]
    ext_ref[halo:halo + c, :] = raw
    conv = convb_ref[...]
    for tap in range(CONV_W - 1):
        off = halo - (CONV_W - 1) + tap
        conv = conv + ext_ref[off:off + c, :] * convw_ref[tap:tap + 1, :]
    conv = conv + raw * convw_ref[CONV_W - 1:CONV_W, :]
    ext_ref[0:halo, :] = ext_ref[c:c + halo, :]
    act = conv * _sigmoid(conv)

    dt = _softplus(dt_ref[0] + dtb_ref[...])
    a = -jnp.exp(alog_ref[...])
    row = lax.broadcasted_iota(jnp.int32, (c, c), 0)
    col = lax.broadcasted_iota(jnp.int32, (c, c), 1)
    tril = row >= col
    cum = jnp.dot(tril.astype(F32), dt * a, precision=lax.Precision.HIGHEST,
                  preferred_element_type=F32)
    hrow = lax.broadcasted_iota(jnp.int32, (SSM_HEADS, SSM_HEADS), 0)
    hcol = lax.broadcasted_iota(jnp.int32, (SSM_HEADS, SSM_HEADS), 1)
    eye_h = (hrow == hcol).astype(F32)
    cum_t = lax.dot_general(eye_h, cum, NT_DIMS, precision=lax.Precision.HIGHEST,
                            preferred_element_type=F32)
    cum_last = cum[c - 1:c, :]
    exp_cum = jnp.exp(cum)
    dt_end = dt * jnp.exp(cum_last - cum)
    chunk_decay = jnp.exp(cum_last)

    nrow = lax.broadcasted_iota(jnp.int32, (n, n), 0)
    ncol = lax.broadcasted_iota(jnp.int32, (n, n), 1)
    eye_n = (nrow == ncol).astype(BF16)
    lane_c = lax.broadcasted_iota(jnp.int32, (c, pair_w), 1) < SSM_HEAD_DIM
    lane_n = lax.broadcasted_iota(jnp.int32, (n, pair_w), 1) < SSM_HEAD_DIM
    lane_1 = lax.broadcasted_iota(jnp.int32, (1, pair_w), 1) < SSM_HEAD_DIM

    heads_per_group = SSM_HEADS // SSM_GROUPS
    for g in range(SSM_GROUPS):
        b_lo = SSM_INNER + g * n
        c_lo = SSM_INNER + SSM_GROUPS * n + g * n
        bm = act[:, b_lo:b_lo + n].astype(BF16)
        cm = act[:, c_lo:c_lo + n]
        cb = lax.dot_general(cm.astype(BF16), bm, NT_DIMS, preferred_element_type=F32)
        bm_t = lax.dot_general(eye_n, bm, NT_DIMS, preferred_element_type=F32).astype(BF16)
        for pair in range(heads_per_group // 2):
            ha = g * heads_per_group + 2 * pair
            hb = ha + 1
            lo = ha * SSM_HEAD_DIM
            x_pair = act[:, lo:lo + pair_w]
            st = ht_ref[0, :, lo:lo + pair_w]
            dt_pair = jnp.where(lane_c, dt[:, ha:ha + 1], dt[:, hb:hb + 1])
            dte_pair = jnp.where(lane_c, dt_end[:, ha:ha + 1], dt_end[:, hb:hb + 1])
            rhs = jnp.concatenate([(x_pair * dt_pair).astype(BF16), st.astype(BF16)], axis=0)
            ys = []
            for h in (ha, hb):
                seg = cum[:, h:h + 1] - cum_t[h:h + 1, :]
                decay = jnp.exp(jnp.where(tril, seg, -jnp.inf))
                lhs = jnp.concatenate(
                    [(cb * decay).astype(BF16), (cm * exp_cum[:, h:h + 1]).astype(BF16)], axis=1)
                ys.append(jnp.dot(lhs, rhs, preferred_element_type=F32))
            y_ref[:, lo:lo + pair_w] = jnp.where(lane_c, ys[0], ys[1])
            upd = jnp.dot(bm_t, (x_pair * dte_pair).astype(BF16), preferred_element_type=F32)
            cd_pair = jnp.where(lane_1, chunk_decay[:, ha:ha + 1], chunk_decay[:, hb:hb + 1])
            ht_ref[0, :, lo:lo + pair_w] = st * cd_pair + upd

    y = y_ref[...] + dskip_ref[...] * act[:, :SSM_INNER]
    zz = z_ref[0]
    gated = y * (zz * _sigmoid(zz))
    yn_ref[0] = ((gated * _rms_scale(gated)) * norm_ref[...]).astype(yn_ref.dtype)


def _ssd(xbc, dt, z, ctx8, h0t, conv_w, conv_b, dt_bias, a_log, d_skip, ssm_norm, c):
    b, l, _ = xbc.shape
    assert l % c == 0 and c % SUBLANES_V7X == 0
    n = SSM_STATE
    per_b = lambda shape: pl.BlockSpec(shape, lambda i, j: (i, 0, 0))
    chunk = lambda w: pl.BlockSpec((1, c, w), lambda i, j: (i, j, 0))
    full = lambda shape: pl.BlockSpec(shape, lambda i, j: (0, 0))
    block_bytes = (_nbytes((c, CONV_DIM), F32) + _nbytes((c, LANES_V7X), F32) + _nbytes((c, SSM_INNER), F32)
                   + _nbytes((8, CONV_DIM), F32) + 2 * _nbytes((n, SSM_INNER), F32)
                   + _nbytes((c, SSM_INNER), BF16) + 8 * _nbytes((8, CONV_DIM), F32))
    scratch_bytes = _nbytes((c + 8, CONV_DIM), F32) + _nbytes((c, SSM_INNER), F32)
    return pl.pallas_call(
        functools.partial(_ssd_kernel, c=c),
        grid=(b, l // c),
        in_specs=[chunk(CONV_DIM), chunk(SSM_HEADS), chunk(SSM_INNER), per_b((1, 8, CONV_DIM)),
                  per_b((1, n, SSM_INNER)), full((CONV_W, CONV_DIM)), full((1, CONV_DIM)),
                  full((1, SSM_HEADS)), full((1, SSM_HEADS)), full((1, SSM_INNER)), full((1, SSM_INNER))],
        out_specs=[chunk(SSM_INNER), per_b((1, n, SSM_INNER))],
        out_shape=[jax.ShapeDtypeStruct((b, l, SSM_INNER), BF16),
                   jax.ShapeDtypeStruct((b, n, SSM_INNER), F32)],
        scratch_shapes=[pltpu.VMEM((c + 8, CONV_DIM), F32), pltpu.VMEM((c, SSM_INNER), F32)],
        compiler_params=pltpu.CompilerParams(
            dimension_semantics=("parallel", "arbitrary"),
            vmem_limit_bytes=_vmem_limit(block_bytes, scratch_bytes)),
        name="ssd",
    )(xbc, dt, z, ctx8, h0t, conv_w, conv_b.reshape(1, -1), dt_bias.reshape(1, -1), a_log.reshape(1, -1),
      jnp.repeat(d_skip, SSM_HEAD_DIM).reshape(1, -1), ssm_norm.reshape(1, -1))


def _sb_attn_kernel(*refs, tq, tkp, n_past, hps, scale):
    if n_past:
        q_ref, kn_ref, vn_ref, kp_ref, vp_ref, o_ref, acc_ref, run_ref = refs
    else:
        q_ref, kn_ref, vn_ref, o_ref, acc_ref, run_ref = refs
    qi = pl.program_id(2)
    width = hps * SB_HEAD_DIM
    m = hps * tq
    q = (q_ref[0].astype(F32) * scale).astype(BF16)
    head_of_lane = lax.broadcasted_iota(jnp.int32, (tq, width), 1) // SB_HEAD_DIM
    zero = jnp.zeros_like(q)
    q_stack = jnp.concatenate([jnp.where(head_of_lane == h, q, zero) for h in range(hps)], axis=0)

    def suffix_ones(t):
        r = lax.broadcasted_iota(jnp.int32, (t, t), 0)
        s = lax.broadcasted_iota(jnp.int32, (t, t), 1)
        return jnp.where(r > s, -1.0, 0.0).astype(BF16)

    acc_ref[...] = jnp.zeros_like(acc_ref)
    run_ref[...] = jnp.zeros_like(run_ref)
    sign_bit = jnp.int32(-2 ** 31)

    def visit(k, v, mask):
        ck = k.shape[0]
        sub = min(ck, SB_SUB)
        ones = suffix_ones(sub)
        zs = lax.dot_general(q_stack, k, NT_DIMS, preferred_element_type=F32)
        neg_abs = lax.bitcast_convert_type(lax.bitcast_convert_type(zs, jnp.int32) | sign_bit, F32)
        drop = jnp.maximum(zs, 0.0) + jnp.log(1.0 + jnp.exp(neg_abs))
        if mask is not None:
            drop = jnp.where(mask, drop, 0.0)
        drop_b = drop.astype(BF16)
        log_beta = zs - drop
        run = run_ref[...]
        ws = []
        for j in reversed(range(ck // sub)):
            sl = slice(j * sub, (j + 1) * sub)
            suffix = jnp.dot(drop_b[:, sl], ones, preferred_element_type=F32)
            w = jnp.exp(log_beta[:, sl] + suffix + run)
            if mask is not None:
                w = jnp.where(mask[:, sl], w, 0.0)
            ws.insert(0, w.astype(BF16))
            first = drop_b[:, j * sub:j * sub + 1].astype(F32)
            run = run + (suffix[:, 0:1] - first)
        w_all = ws[0] if len(ws) == 1 else jnp.concatenate(ws, axis=1)
        acc_ref[...] += jnp.dot(w_all, v, preferred_element_type=F32)
        run_ref[...] = run

    r = lax.rem(lax.broadcasted_iota(jnp.int32, (m, tq), 0), tq)
    s = lax.broadcasted_iota(jnp.int32, (m, tq), 1)
    d0 = pl.multiple_of(qi * tq, tq)
    visit(kn_ref[0, pl.ds(d0, tq), :], vn_ref[0, pl.ds(d0, tq), :], s < r)

    def new_body(i, carry):
        off = pl.multiple_of((qi - 1 - i) * tq, tq)
        visit(kn_ref[0, pl.ds(off, tq), :], vn_ref[0, pl.ds(off, tq), :], None)
        return carry

    lax.fori_loop(0, qi, new_body, 0)

    if n_past:
        def past_body(i, carry):
            off = pl.multiple_of((n_past - 1 - i) * tkp, tkp)
            visit(kp_ref[0, pl.ds(off, tkp), :].astype(BF16), vp_ref[0, pl.ds(off, tkp), :].astype(BF16), None)
            return carry

        lax.fori_loop(0, n_past, past_body, 0)

    out = acc_ref[0:tq, :]
    for h in range(1, hps):
        out = jnp.where(head_of_lane == h, acc_ref[h * tq:(h + 1) * tq, :], out)
    o_ref[0] = out.astype(o_ref.dtype)


def _sb_attn(q, k_new, v_new, k_past, v_past, tq, tkp, hps):
    b, l, _ = q.shape
    assert l % tq == 0 and SB_HEADS % hps == 0
    width = hps * SB_HEAD_DIM
    m = hps * tq
    n_past = 0
    tile = pl.BlockSpec((1, tq, width), lambda i, hq, qi: (i, qi, hq))
    whole = lambda length: pl.BlockSpec((1, length, width), lambda i, hq, qi: (i, 0, hq))
    in_specs = [tile, whole(l), whole(l)]
    args = [q, k_new, v_new]
    block_bytes = 2 * _nbytes((tq, width), BF16) + 2 * _nbytes((l, width), BF16)
    if k_past is not None:
        p = k_past.shape[1]
        assert p % tkp == 0
        n_past = p // tkp
        in_specs += [whole(p), whole(p)]
        args += [k_past, v_past]
        block_bytes += 2 * _nbytes((p, width), F32)
    scratch_bytes = _nbytes((m, width), F32) + _nbytes((m, LANES_V7X), F32)
    return pl.pallas_call(
        functools.partial(_sb_attn_kernel, tq=tq, tkp=tkp, n_past=n_past, hps=hps, scale=SB_HEAD_DIM ** -0.5),
        grid=(b, SB_HEADS // hps, l // tq),
        in_specs=in_specs,
        out_specs=tile,
        out_shape=jax.ShapeDtypeStruct((b, l, SB_INNER), BF16),
        scratch_shapes=[pltpu.VMEM((m, width), F32), pltpu.VMEM((m, 1), F32)],
        compiler_params=pltpu.CompilerParams(
            dimension_semantics=("parallel", "parallel", "arbitrary"),
            vmem_limit_bytes=_vmem_limit(block_bytes, scratch_bytes)),
        name="sb_attn",
    )(*args)


def _merge_mem_kernel(x_ref, yn_ref, ysb_ref, gs_ref, gb_ref, mk_ref, mv_ref, wssm_ref, wsb_ref, wout_ref,
                      nq_ref, wq_ref, wo_ref, o_ref, *, scale):
    y_ssm = jnp.dot(yn_ref[0], wssm_ref[...], preferred_element_type=F32)
    y_sb = jnp.dot(ysb_ref[0], wsb_ref[...], preferred_element_type=F32)
    merged = _sigmoid(gs_ref[0]) * y_ssm + _sigmoid(gb_ref[0]) * y_sb
    x1 = x_ref[0] + jnp.dot(merged.astype(BF16), wout_ref[...], preferred_element_type=F32)
    hq = ((x1 * _rms_scale(x1)) * nq_ref[...]).astype(BF16)
    q = jnp.dot(hq, wq_ref[...], preferred_element_type=F32)
    outs = []
    for h in range(MEM_HEADS):
        sl = slice(h * MEM_HEAD_DIM, (h + 1) * MEM_HEAD_DIM)
        mk = mk_ref[0, :, sl].astype(BF16)
        mv = mv_ref[0, :, sl].astype(BF16)
        s = lax.dot_general(q[:, sl].astype(BF16), mk, NT_DIMS, preferred_element_type=F32) * scale
        e = jnp.exp(s - jnp.max(s, axis=-1, keepdims=True))
        p = e / jnp.sum(e, axis=-1, keepdims=True)
        outs.append(jnp.dot(p.astype(BF16), mv, preferred_element_type=F32))
    o = jnp.concatenate(outs, axis=1)
    o_ref[0] = x1 + jnp.dot(o.astype(BF16), wo_ref[...], preferred_element_type=F32)


def _merge_mem(x, yn, ysb, gs, gb, mk, mv, w_ssm, w_sb, w_out, norm_q, w_q, w_o, tm):
    b, l, d = x.shape
    m = mk.shape[1]
    assert l % tm == 0
    tok = lambda w: pl.BlockSpec((1, tm, w), lambda i, j: (i, j, 0))
    per_b = pl.BlockSpec((1, m, d), lambda i, j: (i, 0, 0))
    full = lambda a: pl.BlockSpec(a.shape, lambda i, j: (0, 0))
    nq = norm_q.reshape(1, d)
    weights = (w_ssm, w_sb, w_out, nq, w_q, w_o)
    block_bytes = (4 * _nbytes((tm, d), F32) + _nbytes((tm, SSM_INNER + SB_INNER), BF16)
                   + 2 * _nbytes((m, d), F32) + sum(_nbytes(w.shape, w.dtype) for w in weights))
    return pl.pallas_call(
        functools.partial(_merge_mem_kernel, scale=MEM_HEAD_DIM ** -0.5),
        grid=(b, l // tm),
        in_specs=[tok(d), tok(SSM_INNER), tok(SB_INNER), tok(d), tok(d), per_b, per_b] + [full(w) for w in weights],
        out_specs=tok(d),
        out_shape=jax.ShapeDtypeStruct((b, l, d), F32),
        compiler_params=pltpu.CompilerParams(
            dimension_semantics=("parallel", "parallel"), vmem_limit_bytes=_vmem_limit(block_bytes)),
        name="merge_mem",
    )(x, yn, ysb, gs, gb, mk, mv, *weights)


def _route(logits):
    lane = lax.broadcasted_iota(jnp.int32, logits.shape, 1)
    big = jnp.int32(LANES_V7X)
    neg = -jnp.inf
    is_group = lane < MOE_GROUPS
    gl = jnp.where(is_group, logits, neg)
    ge = jnp.where(is_group, jnp.exp(gl - jnp.max(gl, axis=-1, keepdims=True)), 0.0)
    pg = ge / jnp.sum(ge, axis=-1, keepdims=True)
    p_sel = jnp.max(pg, axis=-1, keepdims=True)
    g_sel = jnp.min(jnp.where(is_group & (pg == p_sel), lane, big), axis=-1, keepdims=True)
    e_idx = lane - MOE_GROUPS
    in_group = (e_idx >= 0) & (e_idx < N_EXPERTS) & ((e_idx // EXPERTS_PER_GROUP) == g_sel)
    le = jnp.where(in_group, logits, neg)
    v1 = jnp.max(le, axis=-1, keepdims=True)
    i1 = jnp.min(jnp.where(in_group & (le == v1), lane, big), axis=-1, keepdims=True)
    rest = in_group & (lane != i1)
    le2 = jnp.where(rest, logits, neg)
    v2 = jnp.max(le2, axis=-1, keepdims=True)
    i2 = jnp.min(jnp.where(rest & (le2 == v2), lane, big), axis=-1, keepdims=True)
    e2 = jnp.exp(v2 - v1)
    den = 1.0 + e2
    dense_w = (jnp.where(lane == i1, (1.0 / den) * p_sel, 0.0)
               + jnp.where(lane == i2, (e2 / den) * p_sel, 0.0))
    return dense_w, g_sel


def _moe_hidden(x_ref, nf_ref):
    x = x_ref[...]
    return ((x * _rms_scale(x)) * nf_ref[...]).astype(BF16)


def _moe_rank_kernel(x_ref, nf_ref, wr_ref, info_ref, cnt_ref, carry_ref):
    i = pl.program_id(0)

    @pl.when(i == 0)
    def _():
        carry_ref[...] = jnp.zeros_like(carry_ref)

    hb = _moe_hidden(x_ref, nf_ref)
    _, g_sel = _route(jnp.dot(hb, wr_ref[...], preferred_element_type=F32))
    tm = hb.shape[0]
    lane = lax.broadcasted_iota(jnp.int32, (tm, LANES_V7X), 1)
    onehot = lane == g_sel
    r = lax.broadcasted_iota(jnp.int32, (tm, tm), 0)
    c = lax.broadcasted_iota(jnp.int32, (tm, tm), 1)
    before = jnp.dot((c < r).astype(BF16), onehot.astype(BF16), preferred_element_type=F32) + carry_ref[...]
    rank = jnp.sum(jnp.where(onehot, before, 0.0), axis=-1, keepdims=True)
    info = jnp.where(lane == 0, g_sel.astype(F32), jnp.where(lane == 1, rank, 0.0))
    info_ref[...] = info.astype(jnp.int32)
    carry_ref[...] += jnp.sum(onehot.astype(F32), axis=0, keepdims=True)
    cnt_ref[...] = carry_ref[...].astype(jnp.int32)


def _moe_experts_kernel(tile_ref, exp_ref, first_ref, last_ref, act_ref, x_ref, nf_ref, wr_ref, wg_ref, wu_ref,
                        wd_ref, nfin_ref, y_ref, h_ref, dw_ref, acc_ref, *, final_norm):
    w = pl.program_id(0)

    @pl.when(first_ref[w] == 1)
    def _():
        hb = _moe_hidden(x_ref, nf_ref)
        h_ref[...] = hb
        dw_ref[...] = _route(jnp.dot(hb, wr_ref[...], preferred_element_type=F32))[0]
        acc_ref[...] = jnp.zeros_like(acc_ref)

    @pl.when(act_ref[w] == 1)
    def _():
        hb = h_ref[...]
        gate = jnp.dot(hb, wg_ref[0], preferred_element_type=F32)
        up = jnp.dot(hb, wu_ref[0], preferred_element_type=F32)
        he = (gate * _sigmoid(gate)) * up
        down = jnp.dot(he.astype(BF16), wd_ref[0], preferred_element_type=F32)
        lane = lax.broadcasted_iota(jnp.int32, dw_ref.shape, 1)
        dcol = jnp.sum(jnp.where(lane == exp_ref[w] + MOE_GROUPS, dw_ref[...], 0.0), axis=-1, keepdims=True)
        acc_ref[...] += dcol * down

    @pl.when(last_ref[w] == 1)
    def _():
        xo = x_ref[...] + acc_ref[...]
        y_ref[...] = (xo * _rms_scale(xo)) * nfin_ref[...] if final_norm else xo


def _permute_rows_kernel(idx_ref, src_ref, dst_ref, sem, *, rows, scatter):
    def row_copy(r, k):
        src_row, dst_row = (r, k) if scatter else (k, r)
        return pltpu.make_async_copy(src_ref.at[pl.ds(src_row, 1)], dst_ref.at[pl.ds(dst_row, 1)], sem.at[0])

    def start(r, carry):
        row_copy(r, idx_ref[0, 0, r]).start()
        return carry

    def wait(r, carry):
        row_copy(0, 0).wait()
        return carry

    lax.fori_loop(0, rows, start, 0, unroll=8)
    lax.fori_loop(0, rows, wait, 0, unroll=8)


def _permute_rows(src, idx, scatter, name):
    t, d = src.shape
    rows = _tile(t, PERMUTE_ROWS)
    tile = pl.BlockSpec((rows, d), lambda s: (s, 0))
    hbm = pl.BlockSpec(memory_space=pl.ANY)
    return pl.pallas_call(
        functools.partial(_permute_rows_kernel, rows=rows, scatter=scatter),
        grid=(t // rows,),
        in_specs=[pl.BlockSpec((1, 1, rows), lambda s: (s, 0, 0), memory_space=pltpu.SMEM),
                  tile if scatter else hbm],
        out_specs=hbm if scatter else tile,
        out_shape=jax.ShapeDtypeStruct((t, d), src.dtype),
        scratch_shapes=[pltpu.SemaphoreType.DMA((1,))],
        compiler_params=pltpu.CompilerParams(
            dimension_semantics=("arbitrary",), vmem_limit_bytes=_vmem_limit(_nbytes((rows, d), src.dtype))),
        name=name,
    )(idx.reshape(t // rows, 1, rows), src)


def _moe_plan(gid, rank, counts, n_tiles, tm):
    i32 = jnp.int32
    ends = jnp.cumsum(counts).astype(i32)
    pos = (ends - counts)[gid] + rank
    first_row = jnp.arange(n_tiles, dtype=i32) * tm
    groups_before = lambda row: jnp.sum((ends[None, :] <= row[:, None]).astype(i32), axis=1)
    g_lo, g_hi = groups_before(first_row), groups_before(first_row + (tm - 1))
    n_items = EXPERTS_PER_GROUP * (g_hi - g_lo + 1)
    item_end = jnp.cumsum(n_items).astype(i32)
    n_work = EXPERTS_PER_GROUP * (n_tiles + MOE_GROUPS - 1)
    w = jnp.arange(n_work, dtype=i32)
    tile = jnp.minimum(jnp.sum((item_end[None, :] <= w[:, None]).astype(i32), axis=1), n_tiles - 1)
    local = w - (item_end - n_items)[tile]
    active = w < item_end[-1]
    expert = jnp.where(active, EXPERTS_PER_GROUP * g_lo[tile] + local,
                       EXPERTS_PER_GROUP * g_hi[-1] + EXPERTS_PER_GROUP - 1)
    first = active & (local == 0)
    last = active & (local == n_items[tile] - 1)
    return pos.astype(i32), (tile, expert.astype(i32), first.astype(i32), last.astype(i32), active.astype(i32))


def _moe(x2d, norm_ffn, w_router, w_gate, w_up, w_down, norm_final, final_norm, tm):
    t, d = x2d.shape
    assert t % tm == 0 and t < 2 ** 24
    n_tiles = t // tm
    n_e, _, ff = w_gate.shape
    nf, nfin = norm_ffn.reshape(1, d), norm_final.reshape(1, d)
    const = lambda a: pl.BlockSpec(a.shape, lambda i, *_: (0, 0))

    rank_bytes = _nbytes((tm, d), F32) + _nbytes(w_router.shape, BF16) + _nbytes((tm, LANES_V7X), F32)
    info, cnt = pl.pallas_call(
        _moe_rank_kernel,
        grid=(n_tiles,),
        in_specs=[pl.BlockSpec((tm, d), lambda i: (i, 0)), const(nf), const(w_router)],
        out_specs=[pl.BlockSpec((tm, LANES_V7X), lambda i: (i, 0)), pl.BlockSpec((1, LANES_V7X), lambda i: (0, 0))],
        out_shape=[jax.ShapeDtypeStruct((t, LANES_V7X), jnp.int32), jax.ShapeDtypeStruct((1, LANES_V7X), jnp.int32)],
        scratch_shapes=[pltpu.VMEM((1, LANES_V7X), F32)],
        compiler_params=pltpu.CompilerParams(
            dimension_semantics=("arbitrary",), vmem_limit_bytes=_vmem_limit(rank_bytes)),
        name="moe_rank",
    )(x2d, nf, w_router)
    pos, plan = _moe_plan(info[:, 0], info[:, 1], cnt[0, :MOE_GROUPS], n_tiles, tm)

    xs = _permute_rows(x2d, pos, True, "moe_sort")
    by_tile = lambda wi, tile, *_: (tile[wi], 0)
    by_expert = lambda wi, tile, exp, *_: (exp[wi], 0, 0)
    block_bytes = (2 * _nbytes((tm, d), F32) + 2 * _nbytes((1, d), F32) + _nbytes(w_router.shape, BF16)
                   + 3 * _nbytes((d, ff), BF16))
    scratch_bytes = _nbytes((tm, d), BF16) + _nbytes((tm, LANES_V7X), F32) + _nbytes((tm, d), F32)
    ys = pl.pallas_call(
        functools.partial(_moe_experts_kernel, final_norm=final_norm),
        grid_spec=pltpu.PrefetchScalarGridSpec(
            num_scalar_prefetch=len(plan),
            grid=(plan[0].shape[0],),
            in_specs=[pl.BlockSpec((tm, d), by_tile), const(nf), const(w_router),
                      pl.BlockSpec((1, d, ff), by_expert), pl.BlockSpec((1, d, ff), by_expert),
                      pl.BlockSpec((1, ff, d), by_expert), const(nfin)],
            out_specs=pl.BlockSpec((tm, d), by_tile),
            scratch_shapes=[pltpu.VMEM((tm, d), BF16), pltpu.VMEM((tm, LANES_V7X), F32), pltpu.VMEM((tm, d), F32)]),
        out_shape=jax.ShapeDtypeStruct((t, d), F32),
        compiler_params=pltpu.CompilerParams(
            dimension_semantics=("arbitrary",), vmem_limit_bytes=_vmem_limit(block_bytes, scratch_bytes)),
        name="moe_experts",
    )(*plan, xs, nf, w_router, w_gate, w_up, w_down, nfin)
    return _permute_rows(ys, pos, False, "moe_unsort")


def _tile(n, pref):
    t = min(n, pref)
    while n % t:
        t //= 2
    return t


def _split_w_in(w_in):
    offs = [0]
    for s in IN_SIZES:
        offs.append(offs[-1] + s)
    return [w_in[:, offs[i]:offs[i + 1]].astype(BF16) for i in range(len(IN_SIZES))]


def _group_step(x, conv_ctx, ssm_h0, past_k, past_v, mem_k, mem_v, lw):
    b, l, d = x.shape
    t = b * l
    x2d = x.reshape(t, d)
    w_z, w_xbc, w_dt, w_q, w_k, w_v, w_gs, w_gb = lw["w_in_parts"]
    z, xbc, dt, q, k, kb, v, vb, gs, gb = _norm_proj(
        x2d, lw["norm_mix"], [w_z, w_xbc, w_dt, w_q, w_k, w_v, w_gs, w_gb],
        [(F32,), (F32,), (F32,), (BF16,), (F32, BF16), (F32, BF16), (F32,), (F32,)],
        _tile(t, 256), "in_proj")
    r3 = lambda a: a.reshape(b, l, a.shape[-1])

    h0t = ssm_h0.transpose(0, 3, 1, 2).reshape(b, SSM_STATE, SSM_INNER)
    ctx8 = jnp.pad(conv_ctx, ((0, 0), (SUBLANES_V7X - (CONV_W - 1), 0), (0, 0)))
    yn, ht = _ssd(r3(xbc), r3(dt), r3(z), ctx8, h0t, lw["conv_w"], lw["conv_b"], lw["dt_bias"], lw["a_log"],
                  lw["d_skip"], lw["ssm_norm"], _tile(l, 128))
    ssm_new = ht.reshape(b, SSM_STATE, SSM_HEADS, SSM_HEAD_DIM).transpose(0, 2, 3, 1)
    assert l >= CONV_W - 1
    conv_new = r3(xbc)[:, l - (CONV_W - 1):, :]

    tq = _tile(l, SB_CHUNK)
    hps = min(SB_HEADS, SB_STACK_ROWS // tq)
    ysb = _sb_attn(r3(q), r3(kb), r3(vb), past_k, past_v, tq, SB_CHUNK, hps)

    x_mem = _merge_mem(x, yn, ysb, r3(gs), r3(gb), mem_k, mem_v, lw["w_ssm_br"], lw["w_sb_br"], lw["w_out"],
                       lw["norm_mem_q"], lw["w_mem_q"], lw["w_mem_o"], _tile(l, 256))
    kh = k.reshape(b, l, SB_HEADS, SB_HEAD_DIM)
    vh = v.reshape(b, l, SB_HEADS, SB_HEAD_DIM)
    return x_mem, conv_new, ssm_new, kh, vh


def kernel(x_prompt, x_sample, cache_sb_k, cache_sb_v, state_ssm, state_conv, cache_mem_k, cache_mem_v, mem_prompt, norm_mix, w_in, conv_w, conv_b, dt_bias, a_log, d_skip, ssm_norm, w_ssm_br, w_sb_br, w_out, norm_mem_q, norm_mem_kv, w_mem_q, w_mem_kv, w_mem_o, norm_ffn, w_group_router, w_expert_router, w_gate_e, w_up_e, w_down_e, norm_final):
    depth = norm_mix.shape[0]
    bp, lp, d = x_prompt.shape
    bs, ls, _ = x_sample.shape
    xp, xs = x_prompt, x_sample
    outs = {name: [] for name in ("skp", "svp", "ssp", "scp", "mkp", "mvp", "sks", "svs", "sss", "scs")}
    for l in range(depth):
        router = jnp.concatenate([w_group_router[l], w_expert_router[l]], axis=1)
        router = jnp.pad(router, ((0, 0), (0, LANES_V7X - router.shape[1]))).astype(BF16)
        lw = dict(
            norm_mix=norm_mix[l], w_in_parts=_split_w_in(w_in[l]), conv_w=conv_w[l], conv_b=conv_b[l],
            dt_bias=dt_bias[l], a_log=a_log[l], d_skip=d_skip[l], ssm_norm=ssm_norm[l],
            w_ssm_br=w_ssm_br[l].astype(BF16), w_sb_br=w_sb_br[l].astype(BF16), w_out=w_out[l].astype(BF16),
            norm_mem_q=norm_mem_q[l], w_mem_q=w_mem_q[l].astype(BF16), w_mem_o=w_mem_o[l].astype(BF16))
        moe_w = (norm_ffn[l], router, w_gate_e[l].astype(BF16), w_up_e[l].astype(BF16), w_down_e[l].astype(BF16))
        last = l == depth - 1

        m = mem_prompt.shape[1]
        w_kv = w_mem_kv[l].astype(BF16)
        mk_p, mv_p = _norm_proj(mem_prompt.reshape(bp * m, d), norm_mem_kv[l], [w_kv[:, :d], w_kv[:, d:]],
                                [(F32,), (F32,)], _tile(bp * m, 256), "mem_kv")
        mk_p, mv_p = mk_p.reshape(bp, m, d), mv_p.reshape(bp, m, d)
        conv0 = jnp.zeros((bp, CONV_W - 1, CONV_DIM), xp.dtype)
        h0 = jnp.zeros((bp, SSM_HEADS, SSM_HEAD_DIM, SSM_STATE), xp.dtype)
        xp, conv_p, ssm_p, k_p, v_p = _group_step(xp, conv0, h0, None, None, mk_p, mv_p, lw)
        xp = _moe(xp.reshape(bp * lp, d), *moe_w, norm_final, last, _tile(bp * lp, MOE_TILE)).reshape(bp, lp, d)
        outs["skp"].append(k_p); outs["svp"].append(v_p); outs["ssp"].append(ssm_p); outs["scp"].append(conv_p)
        outs["mkp"].append(mk_p.reshape(bp, m, MEM_HEADS, MEM_HEAD_DIM))
        outs["mvp"].append(mv_p.reshape(bp, m, MEM_HEADS, MEM_HEAD_DIM))

        past = cache_sb_k.shape[2]
        ms = cache_mem_k.shape[2]
        xs, conv_s, ssm_s, k_s, v_s = _group_step(
            xs, state_conv[l], state_ssm[l], cache_sb_k[l].reshape(bs, past, SB_INNER),
            cache_sb_v[l].reshape(bs, past, SB_INNER), cache_mem_k[l].reshape(bs, ms, d),
            cache_mem_v[l].reshape(bs, ms, d), lw)
        xs = _moe(xs.reshape(bs * ls, d), *moe_w, norm_final, last, _tile(bs * ls, MOE_TILE)).reshape(bs, ls, d)
        outs["sks"].append(k_s); outs["svs"].append(v_s); outs["sss"].append(ssm_s); outs["scs"].append(conv_s)

    st = lambda name: jnp.stack(outs[name])
    return (xp, xs, st("skp"), st("svp"), st("ssp"), st("scp"), st("mkp"), st("mvp"),
            st("sks"), st("svs"), st("sss"), st("scs"))
```

```python
import functools

import jax
import jax.numpy as jnp
from jax import lax
from jax.experimental import pallas as pl
from jax.experimental.pallas import tpu as pltpu

F32 = jnp.float32
BF16 = jnp.bfloat16
EPS = 1e-6

D_MODEL = 1024
SSM_HEADS = 16
SSM_HEAD_DIM = 64
SSM_INNER = SSM_HEADS * SSM_HEAD_DIM
SSM_GROUPS = 2
SSM_STATE = 128
CONV_W = 4
CONV_DIM = SSM_INNER + 2 * SSM_GROUPS * SSM_STATE
SB_HEADS = 8
SB_HEAD_DIM = 64
SB_INNER = SB_HEADS * SB_HEAD_DIM
MEM_HEADS = 4
MEM_HEAD_DIM = D_MODEL // MEM_HEADS
MOE_GROUPS = 4
EXPERTS_PER_GROUP = 4
N_EXPERTS = MOE_GROUPS * EXPERTS_PER_GROUP
IN_SIZES = (SSM_INNER, CONV_DIM, SSM_HEADS, SB_INNER, SB_INNER, SB_INNER, D_MODEL, D_MODEL)

LANES_V7X = 128
SUBLANES_V7X = 8
VMEM_BYTES_V7X = 64 * 1024 * 1024
COMPILER_TEMP_BYTES = 20 * 1024 * 1024
VMEM_CAP_BYTES = 60000 * 1024

NT_DIMS = (((1,), (1,)), ((), ()))

SB_CHUNK = 512
SB_SUB = 256
SB_STACK_ROWS = 1024

MOE_TILE = 512
PERMUTE_ROWS = 1024


def _vmem_limit(block_bytes, scratch_bytes=0):
    need = 2 * block_bytes + scratch_bytes + COMPILER_TEMP_BYTES
    return int(min(need, VMEM_CAP_BYTES))


def _nbytes(shape, dtype):
    n = 1
    for s in shape:
        n *= s
    return n * jnp.dtype(dtype).itemsize


def _sigmoid(x):
    return 1.0 / (1.0 + jnp.exp(-x))


def _softplus(x):
    return jnp.maximum(x, 0.0) + jnp.log1p(jnp.exp(-jnp.abs(x)))


def _rms_scale(x):
    return lax.rsqrt(jnp.mean(x * x, axis=-1, keepdims=True) + EPS)


def _norm_proj_kernel(x_ref, g_ref, *refs, out_plan):
    n_w = len(out_plan)
    w_refs = refs[:n_w]
    o_refs = refs[n_w:]
    x = x_ref[...]
    h = ((x * _rms_scale(x)) * g_ref[...]).astype(BF16)
    oi = 0
    for w_ref, dtypes in zip(w_refs, out_plan):
        y = jnp.dot(h, w_ref[...], preferred_element_type=F32)
        for dt in dtypes:
            o_refs[oi][...] = y.astype(dt)
            oi += 1


def _norm_proj(x2d, gain, weights, out_plan, tm, name):
    t, d = x2d.shape
    assert t % tm == 0
    in_specs = [pl.BlockSpec((tm, d), lambda i: (i, 0)), pl.BlockSpec((1, d), lambda i: (0, 0))]
    block_bytes = _nbytes((tm, d), F32) + _nbytes((1, d), F32)
    for w in weights:
        in_specs.append(pl.BlockSpec(w.shape, lambda i: (0, 0)))
        block_bytes += _nbytes(w.shape, w.dtype)
    out_shapes, out_specs = [], []
    for w, dtypes in zip(weights, out_plan):
        n = w.shape[1]
        for dt in dtypes:
            out_shapes.append(jax.ShapeDtypeStruct((t, n), dt))
            out_specs.append(pl.BlockSpec((tm, n), lambda i: (i, 0)))
            block_bytes += _nbytes((tm, max(n, LANES_V7X)), dt)
    return pl.pallas_call(
        functools.partial(_norm_proj_kernel, out_plan=tuple(out_plan)),
        grid=(t // tm,),
        in_specs=in_specs,
        out_specs=out_specs,
        out_shape=out_shapes,
        compiler_params=pltpu.CompilerParams(
            dimension_semantics=("parallel",), vmem_limit_bytes=_vmem_limit(block_bytes)),
        name=name,
    )(x2d, gain.reshape(1, d), *weights)


def _ssd_kernel(xbc_ref, dt_ref, z_ref, ctx_ref, h0_ref, convw_ref, convb_ref, dtb_ref, alog_ref,
                dskip_ref, norm_ref, yn_ref, ht_ref, ext_ref, y_ref, *, c):
    j = pl.program_id(1)
    halo = SUBLANES_V7X
    n = SSM_STATE
    pair_w = 2 * SSM_HEAD_DIM

    @pl.when(j == 0)
    def _():
        ext_ref[0:halo, :] = ctx_ref[0]
        ht_ref[0] = h0_ref[0]

    raw = xbc_ref[0]
    ext_ref[halo:halo + c, :] = raw
    ext = ext_ref[...]
    conv = convb_ref[...]
    for tap in range(CONV_W - 1):
        shifted = pltpu.roll(ext, CONV_W - 1 - tap, axis=0)[halo:halo + c, :]
        conv = conv + shifted * convw_ref[tap:tap + 1, :]
    conv = conv + raw * convw_ref[CONV_W - 1:CONV_W, :]
    ext_ref[0:halo, :] = ext_ref[c:c + halo, :]
    act = conv * _sigmoid(conv)

    dt = _softplus(dt_ref[0] + dtb_ref[...])
    a = -jnp.exp(alog_ref[...])
    row = lax.broadcasted_iota(jnp.int32, (c, c), 0)
    col = lax.broadcasted_iota(jnp.int32, (c, c), 1)
    tril = row >= col
    cum = jnp.dot(tril.astype(F32), dt * a, precision=lax.Precision.HIGHEST,
                  preferred_element_type=F32)
    hrow = lax.broadcasted_iota(jnp.int32, (SSM_HEADS, SSM_HEADS), 0)
    hcol = lax.broadcasted_iota(jnp.int32, (SSM_HEADS, SSM_HEADS), 1)
    eye_h = (hrow == hcol).astype(F32)
    cum_t = lax.dot_general(eye_h, cum, NT_DIMS, precision=lax.Precision.HIGHEST,
                            preferred_element_type=F32)
    cum_last = cum[c - 1:c, :]
    exp_cum = jnp.exp(cum)
    dt_end = dt * jnp.exp(cum_last - cum)
    chunk_decay = jnp.exp(cum_last)

    nrow = lax.broadcasted_iota(jnp.int32, (n, n), 0)
    ncol = lax.broadcasted_iota(jnp.int32, (n, n), 1)
    eye_n = (nrow == ncol).astype(BF16)
    lane_c = lax.broadcasted_iota(jnp.int32, (c, pair_w), 1) < SSM_HEAD_DIM
    lane_n = lax.broadcasted_iota(jnp.int32, (n, pair_w), 1) < SSM_HEAD_DIM
    lane_1 = lax.broadcasted_iota(jnp.int32, (1, pair_w), 1) < SSM_HEAD_DIM

    heads_per_group = SSM_HEADS // SSM_GROUPS
    for g in range(SSM_GROUPS):
        b_lo = SSM_INNER + g * n
        c_lo = SSM_INNER + SSM_GROUPS * n + g * n
        bm = act[:, b_lo:b_lo + n].astype(BF16)
        cm = act[:, c_lo:c_lo + n]
        cb = lax.dot_general(cm.astype(BF16), bm, NT_DIMS, preferred_element_type=F32)
        bm_t = lax.dot_general(eye_n, bm, NT_DIMS, preferred_element_type=F32).astype(BF16)
        for pair in range(heads_per_group // 2):
            ha = g * heads_per_group + 2 * pair
            hb = ha + 1
            lo = ha * SSM_HEAD_DIM
            x_pair = act[:, lo:lo + pair_w]
            st = ht_ref[0, :, lo:lo + pair_w]
            dt_pair = jnp.where(lane_c, dt[:, ha:ha + 1], dt[:, hb:hb + 1])
            dte_pair = jnp.where(lane_c, dt_end[:, ha:ha + 1], dt_end[:, hb:hb + 1])
            rhs = jnp.concatenate([(x_pair * dt_pair).astype(BF16), st.astype(BF16)], axis=0)
            ys = []
            for h in (ha, hb):
                seg = cum[:, h:h + 1] - cum_t[h:h + 1, :]
                decay = jnp.exp(jnp.where(tril, seg, -jnp.inf))
                lhs = jnp.concatenate(
                    [(cb * decay).astype(BF16), (cm * exp_cum[:, h:h + 1]).astype(BF16)], axis=1)
                ys.append(jnp.dot(lhs, rhs, preferred_element_type=F32))
            y_ref[:, lo:lo + pair_w] = jnp.where(lane_c, ys[0], ys[1])
            upd = jnp.dot(bm_t, (x_pair * dte_pair).astype(BF16), preferred_element_type=F32)
            cd_pair = jnp.where(lane_1, chunk_decay[:, ha:ha + 1], chunk_decay[:, hb:hb + 1])
            ht_ref[0, :, lo:lo + pair_w] = st * cd_pair + upd

    y = y_ref[...] + dskip_ref[...] * act[:, :SSM_INNER]
    zz = z_ref[0]
    gated = y * (zz * _sigmoid(zz))
    yn_ref[0] = ((gated * _rms_scale(gated)) * norm_ref[...]).astype(yn_ref.dtype)


def _ssd(xbc, dt, z, ctx8, h0t, conv_w, conv_b, dt_bias, a_log, d_skip, ssm_norm, c):
    b, l, _ = xbc.shape
    assert l % c == 0 and c % SUBLANES_V7X == 0
    n = SSM_STATE
    per_b = lambda shape: pl.BlockSpec(shape, lambda i, j: (i, 0, 0))
    chunk = lambda w: pl.BlockSpec((1, c, w), lambda i, j: (i, j, 0))
    full = lambda shape: pl.BlockSpec(shape, lambda i, j: (0, 0))
    block_bytes = (_nbytes((c, CONV_DIM), F32) + _nbytes((c, LANES_V7X), F32) + _nbytes((c, SSM_INNER), F32)
                   + _nbytes((8, CONV_DIM), F32) + 2 * _nbytes((n, SSM_INNER), F32)
                   + _nbytes((c, SSM_INNER), BF16) + 8 * _nbytes((8, CONV_DIM), F32))
    scratch_bytes = _nbytes((c + 8, CONV_DIM), F32) + _nbytes((c, SSM_INNER), F32)
    return pl.pallas_call(
        functools.partial(_ssd_kernel, c=c),
        grid=(b, l // c),
        in_specs=[chunk(CONV_DIM), chunk(SSM_HEADS), chunk(SSM_INNER), per_b((1, 8, CONV_DIM)),
                  per_b((1, n, SSM_INNER)), full((CONV_W, CONV_DIM)), full((1, CONV_DIM)),
                  full((1, SSM_HEADS)), full((1, SSM_HEADS)), full((1, SSM_INNER)), full((1, SSM_INNER))],
        out_specs=[chunk(SSM_INNER), per_b((1, n, SSM_INNER))],
        out_shape=[jax.ShapeDtypeStruct((b, l, SSM_INNER), BF16),
                   jax.ShapeDtypeStruct((b, n, SSM_INNER), F32)],
        scratch_shapes=[pltpu.VMEM((c + 8, CONV_DIM), F32), pltpu.VMEM((c, SSM_INNER), F32)],
        compiler_params=pltpu.CompilerParams(
            dimension_semantics=("parallel", "arbitrary"),
            vmem_limit_bytes=_vmem_limit(block_bytes, scratch_bytes)),
        name="ssd",
    )(xbc, dt, z, ctx8, h0t, conv_w, conv_b.reshape(1, -1), dt_bias.reshape(1, -1), a_log.reshape(1, -1),
      jnp.repeat(d_skip, SSM_HEAD_DIM).reshape(1, -1), ssm_norm.reshape(1, -1))


def _sb_attn_kernel(*refs, tq, tkp, n_past, hps, scale):
    if n_past:
        q_ref, kn_ref, vn_ref, kp_ref, vp_ref, o_ref, acc_ref, run_ref = refs
    else:
        q_ref, kn_ref, vn_ref, o_ref, acc_ref, run_ref, w_ref = refs
    qi = pl.program_id(2)
    width = hps * SB_HEAD_DIM
    m = hps * tq
    q = (q_ref[0].astype(F32) * scale).astype(BF16)
    head_of_lane = lax.broadcasted_iota(jnp.int32, (tq, width), 1) // SB_HEAD_DIM
    zero = jnp.zeros_like(q)
    q_stack = jnp.concatenate([jnp.where(head_of_lane == h, q, zero) for h in range(hps)], axis=0)

    def suffix_ones(t):
        r = lax.broadcasted_iota(jnp.int32, (t, t), 0)
        s = lax.broadcasted_iota(jnp.int32, (t, t), 1)
        return jnp.where(r > s, -1.0, 0.0).astype(BF16)

    acc_ref[...] = jnp.zeros_like(acc_ref)
    run_ref[...] = jnp.zeros_like(run_ref)
    sign_bit = jnp.int32(-2 ** 31)

    def apply(w, v):
        acc_ref[...] += jnp.dot(w, v, preferred_element_type=F32)

    def weights(k, mask, due=None):
        ck = k.shape[0]
        sub = min(ck, SB_SUB)
        ones = suffix_ones(sub)
        zs = lax.dot_general(q_stack, k, NT_DIMS, preferred_element_type=F32)
        neg_abs = lax.bitcast_convert_type(lax.bitcast_convert_type(zs, jnp.int32) | sign_bit, F32)
        drop = jnp.maximum(zs, 0.0) + jnp.log(1.0 + jnp.exp(neg_abs))
        if mask is not None:
            drop = jnp.where(mask, drop, 0.0)
        drop_b = drop.astype(BF16)
        log_beta = zs - drop
        if due is not None:
            apply(*due)
        run = run_ref[...]
        ws = []
        for j in reversed(range(ck // sub)):
            sl = slice(j * sub, (j + 1) * sub)
            suffix = jnp.dot(drop_b[:, sl], ones, preferred_element_type=F32)
            w = jnp.exp(log_beta[:, sl] + suffix + run)
            if mask is not None:
                w = jnp.where(mask[:, sl], w, 0.0)
            ws.insert(0, w.astype(BF16))
            first = drop_b[:, j * sub:j * sub + 1].astype(F32)
            run = run + (suffix[:, 0:1] - first)
        run_ref[...] = run
        return ws[0] if len(ws) == 1 else jnp.concatenate(ws, axis=1)

    def new_chunk(ref, c):
        return ref[0, pl.ds(pl.multiple_of(c * tq, tq), tq), :]

    r = lax.rem(lax.broadcasted_iota(jnp.int32, (m, tq), 0), tq)
    s = lax.broadcasted_iota(jnp.int32, (m, tq), 1)
    w_diag = weights(new_chunk(kn_ref, qi), s < r)

    if n_past:
        apply(w_diag, new_chunk(vn_ref, qi))

        def new_body(i, carry):
            apply(weights(new_chunk(kn_ref, qi - 1 - i), None), new_chunk(vn_ref, qi - 1 - i))
            return carry

        lax.fori_loop(0, qi, new_body, 0)

        def past_body(i, carry):
            off = pl.multiple_of((n_past - 1 - i) * tkp, tkp)
            apply(weights(kp_ref[0, pl.ds(off, tkp), :].astype(BF16), None),
                  vp_ref[0, pl.ds(off, tkp), :].astype(BF16))
            return carry

        lax.fori_loop(0, n_past, past_body, 0)
    else:
        w_ref[...] = w_diag

        def new_body(i, carry):
            due = (w_ref[...], new_chunk(vn_ref, qi - i))
            w_ref[...] = weights(new_chunk(kn_ref, qi - 1 - i), None, due)
            return carry

        lax.fori_loop(0, qi, new_body, 0)
        apply(w_ref[...], new_chunk(vn_ref, 0))

    out = acc_ref[0:tq, :]
    for h in range(1, hps):
        out = jnp.where(head_of_lane == h, acc_ref[h * tq:(h + 1) * tq, :], out)
    o_ref[0] = out.astype(o_ref.dtype)


def _sb_attn(q, k_new, v_new, k_past, v_past, tq, tkp, hps):
    b, l, _ = q.shape
    assert l % tq == 0 and SB_HEADS % hps == 0
    width = hps * SB_HEAD_DIM
    m = hps * tq
    n_past = 0
    tile = pl.BlockSpec((1, tq, width), lambda i, hq, qi: (i, qi, hq))
    whole = lambda length: pl.BlockSpec((1, length, width), lambda i, hq, qi: (i, 0, hq))
    in_specs = [tile, whole(l), whole(l)]
    args = [q, k_new, v_new]
    block_bytes = 2 * _nbytes((tq, width), BF16) + 2 * _nbytes((l, width), BF16)
    if k_past is not None:
        p = k_past.shape[1]
        assert p % tkp == 0
        n_past = p // tkp
        in_specs += [whole(p), whole(p)]
        args += [k_past, v_past]
        block_bytes += 2 * _nbytes((p, width), F32)
    scratch = [pltpu.VMEM((m, width), F32), pltpu.VMEM((m, 1), F32)]
    scratch_bytes = _nbytes((m, width), F32) + _nbytes((m, LANES_V7X), F32)
    if not n_past:
        scratch.append(pltpu.VMEM((m, tq), BF16))
        scratch_bytes += _nbytes((m, tq), BF16)
    return pl.pallas_call(
        functools.partial(_sb_attn_kernel, tq=tq, tkp=tkp, n_past=n_past, hps=hps, scale=SB_HEAD_DIM ** -0.5),
        grid=(b, SB_HEADS // hps, l // tq),
        in_specs=in_specs,
        out_specs=tile,
        out_shape=jax.ShapeDtypeStruct((b, l, SB_INNER), BF16),
        scratch_shapes=scratch,
        compiler_params=pltpu.CompilerParams(
            dimension_semantics=("parallel", "parallel", "arbitrary"),
            vmem_limit_bytes=_vmem_limit(block_bytes, scratch_bytes)),
        name="sb_attn",
    )(*args)


def _merge_mem_kernel(x_ref, yn_ref, ysb_ref, gs_ref, gb_ref, mk_ref, mv_ref, wssm_ref, wsb_ref, wout_ref,
                      nq_ref, wq_ref, wo_ref, o_ref, *, scale):
    y_ssm = jnp.dot(yn_ref[0], wssm_ref[...], preferred_element_type=F32)
    y_sb = jnp.dot(ysb_ref[0], wsb_ref[...], preferred_element_type=F32)
    merged = _sigmoid(gs_ref[0]) * y_ssm + _sigmoid(gb_ref[0]) * y_sb
    x1 = x_ref[0] + jnp.dot(merged.astype(BF16), wout_ref[...], preferred_element_type=F32)
    hq = ((x1 * _rms_scale(x1)) * nq_ref[...]).astype(BF16)
    q = jnp.dot(hq, wq_ref[...], preferred_element_type=F32)
    outs = []
    for h in range(MEM_HEADS):
        sl = slice(h * MEM_HEAD_DIM, (h + 1) * MEM_HEAD_DIM)
        mk = mk_ref[0, :, sl].astype(BF16)
        mv = mv_ref[0, :, sl].astype(BF16)
        s = lax.dot_general(q[:, sl].astype(BF16), mk, NT_DIMS, preferred_element_type=F32) * scale
        e = jnp.exp(s - jnp.max(s, axis=-1, keepdims=True))
        p = e / jnp.sum(e, axis=-1, keepdims=True)
        outs.append(jnp.dot(p.astype(BF16), mv, preferred_element_type=F32))
    o = jnp.concatenate(outs, axis=1)
    o_ref[0] = x1 + jnp.dot(o.astype(BF16), wo_ref[...], preferred_element_type=F32)


def _merge_mem(x, yn, ysb, gs, gb, mk, mv, w_ssm, w_sb, w_out, norm_q, w_q, w_o, tm):
    b, l, d = x.shape
    m = mk.shape[1]
    assert l % tm == 0
    tok = lambda w: pl.BlockSpec((1, tm, w), lambda i, j: (i, j, 0))
    per_b = pl.BlockSpec((1, m, d), lambda i, j: (i, 0, 0))
    full = lambda a: pl.BlockSpec(a.shape, lambda i, j: (0, 0))
    nq = norm_q.reshape(1, d)
    weights = (w_ssm, w_sb, w_out, nq, w_q, w_o)
    block_bytes = (4 * _nbytes((tm, d), F32) + _nbytes((tm, SSM_INNER + SB_INNER), BF16)
                   + 2 * _nbytes((m, d), F32) + sum(_nbytes(w.shape, w.dtype) for w in weights))
    return pl.pallas_call(
        functools.partial(_merge_mem_kernel, scale=MEM_HEAD_DIM ** -0.5),
        grid=(b, l // tm),
        in_specs=[tok(d), tok(SSM_INNER), tok(SB_INNER), tok(d), tok(d), per_b, per_b] + [full(w) for w in weights],
        out_specs=tok(d),
        out_shape=jax.ShapeDtypeStruct((b, l, d), F32),
        compiler_params=pltpu.CompilerParams(
            dimension_semantics=("parallel", "parallel"), vmem_limit_bytes=_vmem_limit(block_bytes)),
        name="merge_mem",
    )(x, yn, ysb, gs, gb, mk, mv, *weights)


def _route(logits):
    lane = lax.broadcasted_iota(jnp.int32, logits.shape, 1)
    big = jnp.int32(LANES_V7X)
    neg = -jnp.inf
    is_group = lane < MOE_GROUPS
    gl = jnp.where(is_group, logits, neg)
    ge = jnp.where(is_group, jnp.exp(gl - jnp.max(gl, axis=-1, keepdims=True)), 0.0)
    pg = ge / jnp.sum(ge, axis=-1, keepdims=True)
    p_sel = jnp.max(pg, axis=-1, keepdims=True)
    g_sel = jnp.min(jnp.where(is_group & (pg == p_sel), lane, big), axis=-1, keepdims=True)
    e_idx = lane - MOE_GROUPS
    in_group = (e_idx >= 0) & (e_idx < N_EXPERTS) & ((e_idx // EXPERTS_PER_GROUP) == g_sel)
    le = jnp.where(in_group, logits, neg)
    v1 = jnp.max(le, axis=-1, keepdims=True)
    i1 = jnp.min(jnp.where(in_group & (le == v1), lane, big), axis=-1, keepdims=True)
    rest = in_group & (lane != i1)
    le2 = jnp.where(rest, logits, neg)
    v2 = jnp.max(le2, axis=-1, keepdims=True)
    i2 = jnp.min(jnp.where(rest & (le2 == v2), lane, big), axis=-1, keepdims=True)
    e2 = jnp.exp(v2 - v1)
    den = 1.0 + e2
    dense_w = (jnp.where(lane == i1, (1.0 / den) * p_sel, 0.0)
               + jnp.where(lane == i2, (e2 / den) * p_sel, 0.0))
    return dense_w, g_sel


def _moe_hidden(x_ref, nf_ref):
    x = x_ref[...]
    return ((x * _rms_scale(x)) * nf_ref[...]).astype(BF16)


def _moe_rank_kernel(x_ref, nf_ref, wr_ref, info_ref, cnt_ref, carry_ref):
    i = pl.program_id(0)

    @pl.when(i == 0)
    def _():
        carry_ref[...] = jnp.zeros_like(carry_ref)

    hb = _moe_hidden(x_ref, nf_ref)
    _, g_sel = _route(jnp.dot(hb, wr_ref[...], preferred_element_type=F32))
    tm = hb.shape[0]
    lane = lax.broadcasted_iota(jnp.int32, (tm, LANES_V7X), 1)
    onehot = lane == g_sel
    r = lax.broadcasted_iota(jnp.int32, (tm, tm), 0)
    c = lax.broadcasted_iota(jnp.int32, (tm, tm), 1)
    before = jnp.dot((c < r).astype(BF16), onehot.astype(BF16), preferred_element_type=F32) + carry_ref[...]
    rank = jnp.sum(jnp.where(onehot, before, 0.0), axis=-1, keepdims=True)
    info = jnp.where(lane == 0, g_sel.astype(F32), jnp.where(lane == 1, rank, 0.0))
    info_ref[...] = info.astype(jnp.int32)
    carry_ref[...] += jnp.sum(onehot.astype(F32), axis=0, keepdims=True)
    cnt_ref[...] = carry_ref[...].astype(jnp.int32)


def _moe_experts_kernel(tile_ref, exp_ref, first_ref, last_ref, act_ref, x_ref, nf_ref, wr_ref, wg_ref, wu_ref,
                        wd_ref, nfin_ref, y_ref, h_ref, dw_ref, acc_ref, *, final_norm):
    w = pl.program_id(0)

    @pl.when(first_ref[w] == 1)
    def _():
        hb = _moe_hidden(x_ref, nf_ref)
        h_ref[...] = hb
        dw_ref[...] = _route(jnp.dot(hb, wr_ref[...], preferred_element_type=F32))[0]
        acc_ref[...] = jnp.zeros_like(acc_ref)

    @pl.when(act_ref[w] == 1)
    def _():
        hb = h_ref[...]
        gate = jnp.dot(hb, wg_ref[0], preferred_element_type=F32)
        up = jnp.dot(hb, wu_ref[0], preferred_element_type=F32)
        he = (gate * _sigmoid(gate)) * up
        down = jnp.dot(he.astype(BF16), wd_ref[0], preferred_element_type=F32)
        lane = lax.broadcasted_iota(jnp.int32, dw_ref.shape, 1)
        dcol = jnp.sum(jnp.where(lane == exp_ref[w] + MOE_GROUPS, dw_ref[...], 0.0), axis=-1, keepdims=True)
        acc_ref[...] += dcol * down

    @pl.when(last_ref[w] == 1)
    def _():
        xo = x_ref[...] + acc_ref[...]
        y_ref[...] = (xo * _rms_scale(xo)) * nfin_ref[...] if final_norm else xo


def _permute_rows_kernel(idx_ref, src_ref, dst_ref, sem, *, rows, scatter):
    def row_copy(r, k):
        src_row, dst_row = (r, k) if scatter else (k, r)
        return pltpu.make_async_copy(src_ref.at[pl.ds(src_row, 1)], dst_ref.at[pl.ds(dst_row, 1)], sem.at[0])

    def start(r, carry):
        row_copy(r, idx_ref[0, 0, r]).start()
        return carry

    def wait(r, carry):
        row_copy(0, 0).wait()
        return carry

    lax.fori_loop(0, rows, start, 0, unroll=8)
    lax.fori_loop(0, rows, wait, 0, unroll=8)


def _permute_rows(src, idx, scatter, name):
    t, d = src.shape
    rows = _tile(t, PERMUTE_ROWS)
    tile = pl.BlockSpec((rows, d), lambda s: (s, 0))
    hbm = pl.BlockSpec(memory_space=pl.ANY)
    return pl.pallas_call(
        functools.partial(_permute_rows_kernel, rows=rows, scatter=scatter),
        grid=(t // rows,),
        in_specs=[pl.BlockSpec((1, 1, rows), lambda s: (s, 0, 0), memory_space=pltpu.SMEM),
                  tile if scatter else hbm],
        out_specs=hbm if scatter else tile,
        out_shape=jax.ShapeDtypeStruct((t, d), src.dtype),
        scratch_shapes=[pltpu.SemaphoreType.DMA((1,))],
        compiler_params=pltpu.CompilerParams(
            dimension_semantics=("arbitrary",), vmem_limit_bytes=_vmem_limit(_nbytes((rows, d), src.dtype))),
        name=name,
    )(idx.reshape(t // rows, 1, rows), src)


def _moe_plan(gid, rank, counts, n_tiles, tm):
    i32 = jnp.int32
    ends = jnp.cumsum(counts).astype(i32)
    pos = (ends - counts)[gid] + rank
    first_row = jnp.arange(n_tiles, dtype=i32) * tm
    groups_before = lambda row: jnp.sum((ends[None, :] <= row[:, None]).astype(i32), axis=1)
    g_lo, g_hi = groups_before(first_row), groups_before(first_row + (tm - 1))
    n_items = EXPERTS_PER_GROUP * (g_hi - g_lo + 1)
    item_end = jnp.cumsum(n_items).astype(i32)
    n_work = EXPERTS_PER_GROUP * (n_tiles + MOE_GROUPS - 1)
    w = jnp.arange(n_work, dtype=i32)
    tile = jnp.minimum(jnp.sum((item_end[None, :] <= w[:, None]).astype(i32), axis=1), n_tiles - 1)
    local = w - (item_end - n_items)[tile]
    active = w < item_end[-1]
    expert = jnp.where(active, EXPERTS_PER_GROUP * g_lo[tile] + local,
                       EXPERTS_PER_GROUP * g_hi[-1] + EXPERTS_PER_GROUP - 1)
    first = active & (local == 0)
    last = active & (local == n_items[tile] - 1)
    return pos.astype(i32), (tile, expert.astype(i32), first.astype(i32), last.astype(i32), active.astype(i32))


def _moe(x2d, norm_ffn, w_router, w_gate, w_up, w_down, norm_final, final_norm, tm):
    t, d = x2d.shape
    assert t % tm == 0 and t < 2 ** 24
    n_tiles = t // tm
    n_e, _, ff = w_gate.shape
    nf, nfin = norm_ffn.reshape(1, d), norm_final.reshape(1, d)
    const = lambda a: pl.BlockSpec(a.shape, lambda i, *_: (0, 0))

    rank_bytes = _nbytes((tm, d), F32) + _nbytes(w_router.shape, BF16) + _nbytes((tm, LANES_V7X), F32)
    info, cnt = pl.pallas_call(
        _moe_rank_kernel,
        grid=(n_tiles,),
        in_specs=[pl.BlockSpec((tm, d), lambda i: (i, 0)), const(nf), const(w_router)],
        out_specs=[pl.BlockSpec((tm, LANES_V7X), lambda i: (i, 0)), pl.BlockSpec((1, LANES_V7X), lambda i: (0, 0))],
        out_shape=[jax.ShapeDtypeStruct((t, LANES_V7X), jnp.int32), jax.ShapeDtypeStruct((1, LANES_V7X), jnp.int32)],
        scratch_shapes=[pltpu.VMEM((1, LANES_V7X), F32)],
        compiler_params=pltpu.CompilerParams(
            dimension_semantics=("arbitrary",), vmem_limit_bytes=_vmem_limit(rank_bytes)),
        name="moe_rank",
    )(x2d, nf, w_router)
    pos, plan = _moe_plan(info[:, 0], info[:, 1], cnt[0, :MOE_GROUPS], n_tiles, tm)

    xs = _permute_rows(x2d, pos, True, "moe_sort")
    by_tile = lambda wi, tile, *_: (tile[wi], 0)
    by_expert = lambda wi, tile, exp, *_: (exp[wi], 0, 0)
    block_bytes = (2 * _nbytes((tm, d), F32) + 2 * _nbytes((1, d), F32) + _nbytes(w_router.shape, BF16)
                   + 3 * _nbytes((d, ff), BF16))
    scratch_bytes = _nbytes((tm, d), BF16) + _nbytes((tm, LANES_V7X), F32) + _nbytes((tm, d), F32)
    ys = pl.pallas_call(
        functools.partial(_moe_experts_kernel, final_norm=final_norm),
        grid_spec=pltpu.PrefetchScalarGridSpec(
            num_scalar_prefetch=len(plan),
            grid=(plan[0].shape[0],),
            in_specs=[pl.BlockSpec((tm, d), by_tile), const(nf), const(w_router),
                      pl.BlockSpec((1, d, ff), by_expert), pl.BlockSpec((1, d, ff), by_expert),
                      pl.BlockSpec((1, ff, d), by_expert), const(nfin)],
            out_specs=pl.BlockSpec((tm, d), by_tile),
            scratch_shapes=[pltpu.VMEM((tm, d), BF16), pltpu.VMEM((tm, LANES_V7X), F32), pltpu.VMEM((tm, d), F32)]),
        out_shape=jax.ShapeDtypeStruct((t, d), F32),
        compiler_params=pltpu.CompilerParams(
            dimension_semantics=("arbitrary",), vmem_limit_bytes=_vmem_limit(block_bytes, scratch_bytes)),
        name="moe_experts",
    )(*plan, xs, nf, w_router, w_gate, w_up, w_down, nfin)
    return _permute_rows(ys, pos, False, "moe_unsort")


def _tile(n, pref):
    t = min(n, pref)
    while n % t:
        t //= 2
    return t


def _split_w_in(w_in):
    offs = [0]
    for s in IN_SIZES:
        offs.append(offs[-1] + s)
    return [w_in[:, offs[i]:offs[i + 1]].astype(BF16) for i in range(len(IN_SIZES))]


def _group_step(x, conv_ctx, ssm_h0, past_k, past_v, mem_k, mem_v, lw):
    b, l, d = x.shape
    t = b * l
    x2d = x.reshape(t, d)
    w_z, w_xbc, w_dt, w_q, w_k, w_v, w_gs, w_gb = lw["w_in_parts"]
    z, xbc, dt, q, k, kb, v, vb, gs, gb = _norm_proj(
        x2d, lw["norm_mix"], [w_z, w_xbc, w_dt, w_q, w_k, w_v, w_gs, w_gb],
        [(F32,), (F32,), (F32,), (BF16,), (F32, BF16), (F32, BF16), (F32,), (F32,)],
        _tile(t, 256), "in_proj")
    r3 = lambda a: a.reshape(b, l, a.shape[-1])

    h0t = ssm_h0.transpose(0, 3, 1, 2).reshape(b, SSM_STATE, SSM_INNER)
    ctx8 = jnp.pad(conv_ctx, ((0, 0), (SUBLANES_V7X - (CONV_W - 1), 0), (0, 0)))
    yn, ht = _ssd(r3(xbc), r3(dt), r3(z), ctx8, h0t, lw["conv_w"], lw["conv_b"], lw["dt_bias"], lw["a_log"],
                  lw["d_skip"], lw["ssm_norm"], _tile(l, 128))
    ssm_new = ht.reshape(b, SSM_STATE, SSM_HEADS, SSM_HEAD_DIM).transpose(0, 2, 3, 1)
    assert l >= CONV_W - 1
    conv_new = r3(xbc)[:, l - (CONV_W - 1):, :]

    tq = _tile(l, SB_CHUNK)
    hps = min(SB_HEADS, SB_STACK_ROWS // tq)
    ysb = _sb_attn(r3(q), r3(kb), r3(vb), past_k, past_v, tq, SB_CHUNK, hps)

    x_mem = _merge_mem(x, yn, ysb, r3(gs), r3(gb), mem_k, mem_v, lw["w_ssm_br"], lw["w_sb_br"], lw["w_out"],
                       lw["norm_mem_q"], lw["w_mem_q"], lw["w_mem_o"], _tile(l, 512))
    kh = k.reshape(b, l, SB_HEADS, SB_HEAD_DIM)
    vh = v.reshape(b, l, SB_HEADS, SB_HEAD_DIM)
    return x_mem, conv_new, ssm_new, kh, vh


def kernel(x_prompt, x_sample, cache_sb_k, cache_sb_v, state_ssm, state_conv, cache_mem_k, cache_mem_v, mem_prompt, norm_mix, w_in, conv_w, conv_b, dt_bias, a_log, d_skip, ssm_norm, w_ssm_br, w_sb_br, w_out, norm_mem_q, norm_mem_kv, w_mem_q, w_mem_kv, w_mem_o, norm_ffn, w_group_router, w_expert_router, w_gate_e, w_up_e, w_down_e, norm_final):
    depth = norm_mix.shape[0]
    bp, lp, d = x_prompt.shape
    bs, ls, _ = x_sample.shape
    xp, xs = x_prompt, x_sample
    outs = {name: [] for name in ("skp", "svp", "ssp", "scp", "mkp", "mvp", "sks", "svs", "sss", "scs")}
    for l in range(depth):
        router = jnp.concatenate([w_group_router[l], w_expert_router[l]], axis=1)
        router = jnp.pad(router, ((0, 0), (0, LANES_V7X - router.shape[1]))).astype(BF16)
        lw = dict(
            norm_mix=norm_mix[l], w_in_parts=_split_w_in(w_in[l]), conv_w=conv_w[l], conv_b=conv_b[l],
            dt_bias=dt_bias[l], a_log=a_log[l], d_skip=d_skip[l], ssm_norm=ssm_norm[l],
            w_ssm_br=w_ssm_br[l].astype(BF16), w_sb_br=w_sb_br[l].astype(BF16), w_out=w_out[l].astype(BF16),
            norm_mem_q=norm_mem_q[l], w_mem_q=w_mem_q[l].astype(BF16), w_mem_o=w_mem_o[l].astype(BF16))
        moe_w = (norm_ffn[l], router, w_gate_e[l].astype(BF16), w_up_e[l].astype(BF16), w_down_e[l].astype(BF16))
        last = l == depth - 1

        m = mem_prompt.shape[1]
        w_kv = w_mem_kv[l].astype(BF16)
        mk_p, mv_p = _norm_proj(mem_prompt.reshape(bp * m, d), norm_mem_kv[l], [w_kv[:, :d], w_kv[:, d:]],
                                [(F32,), (F32,)], _tile(bp * m, 256), "mem_kv")
        mk_p, mv_p = mk_p.reshape(bp, m, d), mv_p.reshape(bp, m, d)
        conv0 = jnp.zeros((bp, CONV_W - 1, CONV_DIM), xp.dtype)
        h0 = jnp.zeros((bp, SSM_HEADS, SSM_HEAD_DIM, SSM_STATE), xp.dtype)
        xp, conv_p, ssm_p, k_p, v_p = _group_step(xp, conv0, h0, None, None, mk_p, mv_p, lw)
        xp = _moe(xp.reshape(bp * lp, d), *moe_w, norm_final, last, _tile(bp * lp, MOE_TILE)).reshape(bp, lp, d)
        outs["skp"].append(k_p); outs["svp"].append(v_p); outs["ssp"].append(ssm_p); outs["scp"].append(conv_p)
        outs["mkp"].append(mk_p.reshape(bp, m, MEM_HEADS, MEM_HEAD_DIM))
        outs["mvp"].append(mv_p.reshape(bp, m, MEM_HEADS, MEM_HEAD_DIM))

        past = cache_sb_k.shape[2]
        ms = cache_mem_k.shape[2]
        xs, conv_s, ssm_s, k_s, v_s = _group_step(
            xs, state_conv[l], state_ssm[l], cache_sb_k[l].reshape(bs, past, SB_INNER),
            cache_sb_v[l].reshape(bs, past, SB_INNER), cache_mem_k[l].reshape(bs, ms, d),
            cache_mem_v[l].reshape(bs, ms, d), lw)
        xs = _moe(xs.reshape(bs * ls, d), *moe_w, norm_final, last, _tile(bs * ls, MOE_TILE)).reshape(bs, ls, d)
        outs["sks"].append(k_s); outs["svs"].append(v_s); outs["sss"].append(ssm_s); outs["scs"].append(conv_s)

    st = lambda name: jnp.stack(outs[name])
    return (xp, xs, st("skp"), st("svp"), st("ssp"), st("scp"), st("mkp"), st("mvp"),
            st("sks"), st("svs"), st("sss"), st("scs"))
```

```python
import functools

import jax
import jax.numpy as jnp
from jax import lax
from jax.experimental import pallas as pl
from jax.experimental.pallas import tpu as pltpu

F32 = jnp.float32
BF16 = jnp.bfloat16
EPS = 1e-6

D_MODEL = 1024
SSM_HEADS = 16
SSM_HEAD_DIM = 64
SSM_INNER = SSM_HEADS * SSM_HEAD_DIM
SSM_GROUPS = 2
SSM_STATE = 128
CONV_W = 4
CONV_DIM = SSM_INNER + 2 * SSM_GROUPS * SSM_STATE
SB_HEADS = 8
SB_HEAD_DIM = 64
SB_INNER = SB_HEADS * SB_HEAD_DIM
MEM_HEADS = 4
MEM_HEAD_DIM = D_MODEL // MEM_HEADS
MOE_GROUPS = 4
EXPERTS_PER_GROUP = 4
N_EXPERTS = MOE_GROUPS * EXPERTS_PER_GROUP
IN_SIZES = (SSM_INNER, CONV_DIM, SSM_HEADS, SB_INNER, SB_INNER, SB_INNER, D_MODEL, D_MODEL)

LANES_V7X = 128
SUBLANES_V7X = 8
VMEM_BYTES_V7X = 64 * 1024 * 1024
COMPILER_TEMP_BYTES = 20 * 1024 * 1024
VMEM_CAP_BYTES = 60000 * 1024

NT_DIMS = (((1,), (1,)), ((), ()))

SB_CHUNK = 512
SB_SUB = 256
SB_STACK_ROWS = 2048

MOE_TILE = 512
PERMUTE_ROWS = 2048


def _vmem_limit(block_bytes, scratch_bytes=0):
    need = 2 * block_bytes + scratch_bytes + COMPILER_TEMP_BYTES
    return int(min(need, VMEM_CAP_BYTES))


def _nbytes(shape, dtype):
    n = 1
    for s in shape:
        n *= s
    return n * jnp.dtype(dtype).itemsize


def _sigmoid(x):
    return 1.0 / (1.0 + jnp.exp(-x))


def _softplus(x):
    return jnp.maximum(x, 0.0) + jnp.log1p(jnp.exp(-jnp.abs(x)))


def _rms_scale(x):
    return lax.rsqrt(jnp.mean(x * x, axis=-1, keepdims=True) + EPS)


def _norm_proj_kernel(x_ref, g_ref, *refs, out_plan):
    n_w = len(out_plan)
    w_refs = refs[:n_w]
    o_refs = refs[n_w:]
    x = x_ref[...]
    h = ((x * _rms_scale(x)) * g_ref[...]).astype(BF16)
    oi = 0
    for w_ref, dtypes in zip(w_refs, out_plan):
        y = jnp.dot(h, w_ref[...], preferred_element_type=F32)
        for dt in dtypes:
            o_refs[oi][...] = y.astype(dt)
            oi += 1


def _norm_proj(x2d, gain, weights, out_plan, tm, name):
    t, d = x2d.shape
    assert t % tm == 0
    in_specs = [pl.BlockSpec((tm, d), lambda i: (i, 0)), pl.BlockSpec((1, d), lambda i: (0, 0))]
    block_bytes = _nbytes((tm, d), F32) + _nbytes((1, d), F32)
    for w in weights:
        in_specs.append(pl.BlockSpec(w.shape, lambda i: (0, 0)))
        block_bytes += _nbytes(w.shape, w.dtype)
    out_shapes, out_specs = [], []
    for w, dtypes in zip(weights, out_plan):
        n = w.shape[1]
        for dt in dtypes:
            out_shapes.append(jax.ShapeDtypeStruct((t, n), dt))
            out_specs.append(pl.BlockSpec((tm, n), lambda i: (i, 0)))
            block_bytes += _nbytes((tm, max(n, LANES_V7X)), dt)
    return pl.pallas_call(
        functools.partial(_norm_proj_kernel, out_plan=tuple(out_plan)),
        grid=(t // tm,),
        in_specs=in_specs,
        out_specs=out_specs,
        out_shape=out_shapes,
        compiler_params=pltpu.CompilerParams(
            dimension_semantics=("parallel",), vmem_limit_bytes=_vmem_limit(block_bytes)),
        name=name,
    )(x2d, gain.reshape(1, d), *weights)


def _ssd_kernel(xbc_ref, dt_ref, z_ref, ctx_ref, h0_ref, convw_ref, convb_ref, dtb_ref, alog_ref,
                dskip_ref, norm_ref, yn_ref, ht_ref, ext_ref, y_ref, *, c):
    j = pl.program_id(1)
    halo = SUBLANES_V7X
    n = SSM_STATE
    pair_w = 2 * SSM_HEAD_DIM

    @pl.when(j == 0)
    def _():
        ext_ref[0:halo, :] = ctx_ref[0]
        ht_ref[0] = h0_ref[0]

    raw = xbc_ref[0]
    ext_ref[halo:halo + c, :] = raw
    ext = ext_ref[...]
    conv = convb_ref[...]
    for tap in range(CONV_W - 1):
        shifted = pltpu.roll(ext, CONV_W - 1 - tap, axis=0)[halo:halo + c, :]
        conv = conv + shifted * convw_ref[tap:tap + 1, :]
    conv = conv + raw * convw_ref[CONV_W - 1:CONV_W, :]
    ext_ref[0:halo, :] = ext_ref[c:c + halo, :]
    act = conv * _sigmoid(conv)

    dt = _softplus(dt_ref[0] + dtb_ref[...])
    a = -jnp.exp(alog_ref[...])
    row = lax.broadcasted_iota(jnp.int32, (c, c), 0)
    col = lax.broadcasted_iota(jnp.int32, (c, c), 1)
    tril = row >= col
    cum = jnp.dot(tril.astype(F32), dt * a, precision=lax.Precision.HIGHEST,
                  preferred_element_type=F32)
    hrow = lax.broadcasted_iota(jnp.int32, (SSM_HEADS, SSM_HEADS), 0)
    hcol = lax.broadcasted_iota(jnp.int32, (SSM_HEADS, SSM_HEADS), 1)
    eye_h = (hrow == hcol).astype(F32)
    cum_t = lax.dot_general(eye_h, cum, NT_DIMS, precision=lax.Precision.HIGHEST,
                            preferred_element_type=F32)
    cum_last = cum[c - 1:c, :]
    exp_cum = jnp.exp(cum)
    dt_end = dt * jnp.exp(cum_last - cum)
    chunk_decay = jnp.exp(cum_last)

    nrow = lax.broadcasted_iota(jnp.int32, (n, n), 0)
    ncol = lax.broadcasted_iota(jnp.int32, (n, n), 1)
    eye_n = (nrow == ncol).astype(BF16)
    lane_c = lax.broadcasted_iota(jnp.int32, (c, pair_w), 1) < SSM_HEAD_DIM
    lane_n = lax.broadcasted_iota(jnp.int32, (n, pair_w), 1) < SSM_HEAD_DIM
    lane_1 = lax.broadcasted_iota(jnp.int32, (1, pair_w), 1) < SSM_HEAD_DIM

    heads_per_group = SSM_HEADS // SSM_GROUPS
    for g in range(SSM_GROUPS):
        b_lo = SSM_INNER + g * n
        c_lo = SSM_INNER + SSM_GROUPS * n + g * n
        bm = act[:, b_lo:b_lo + n].astype(BF16)
        cm = act[:, c_lo:c_lo + n]
        cb = lax.dot_general(cm.astype(BF16), bm, NT_DIMS, preferred_element_type=F32)
        bm_t = lax.dot_general(eye_n, bm, NT_DIMS, preferred_element_type=F32).astype(BF16)
        for pair in range(heads_per_group // 2):
            ha = g * heads_per_group + 2 * pair
            hb = ha + 1
            lo = ha * SSM_HEAD_DIM
            x_pair = act[:, lo:lo + pair_w]
            st = ht_ref[0, :, lo:lo + pair_w]
            dt_pair = jnp.where(lane_c, dt[:, ha:ha + 1], dt[:, hb:hb + 1])
            dte_pair = jnp.where(lane_c, dt_end[:, ha:ha + 1], dt_end[:, hb:hb + 1])
            rhs = jnp.concatenate([(x_pair * dt_pair).astype(BF16), st.astype(BF16)], axis=0)
            ys = []
            for h in (ha, hb):
                seg = cum[:, h:h + 1] - cum_t[h:h + 1, :]
                decay = jnp.exp(jnp.where(tril, seg, -jnp.inf))
                lhs = jnp.concatenate(
                    [(cb * decay).astype(BF16), (cm * exp_cum[:, h:h + 1]).astype(BF16)], axis=1)
                ys.append(jnp.dot(lhs, rhs, preferred_element_type=F32))
            y_ref[:, lo:lo + pair_w] = jnp.where(lane_c, ys[0], ys[1])
            upd = jnp.dot(bm_t, (x_pair * dte_pair).astype(BF16), preferred_element_type=F32)
            cd_pair = jnp.where(lane_1, chunk_decay[:, ha:ha + 1], chunk_decay[:, hb:hb + 1])
            ht_ref[0, :, lo:lo + pair_w] = st * cd_pair + upd

    y = y_ref[...] + dskip_ref[...] * act[:, :SSM_INNER]
    zz = z_ref[0]
    gated = y * (zz * _sigmoid(zz))
    yn_ref[0] = ((gated * _rms_scale(gated)) * norm_ref[...]).astype(yn_ref.dtype)


def _ssd(xbc, dt, z, ctx8, h0t, conv_w, conv_b, dt_bias, a_log, d_skip, ssm_norm, c):
    b, l, _ = xbc.shape
    assert l % c == 0 and c % SUBLANES_V7X == 0
    n = SSM_STATE
    per_b = lambda shape: pl.BlockSpec(shape, lambda i, j: (i, 0, 0))
    chunk = lambda w: pl.BlockSpec((1, c, w), lambda i, j: (i, j, 0))
    full = lambda shape: pl.BlockSpec(shape, lambda i, j: (0, 0))
    block_bytes = (_nbytes((c, CONV_DIM), F32) + _nbytes((c, LANES_V7X), F32) + _nbytes((c, SSM_INNER), F32)
                   + _nbytes((8, CONV_DIM), F32) + 2 * _nbytes((n, SSM_INNER), F32)
                   + _nbytes((c, SSM_INNER), BF16) + 8 * _nbytes((8, CONV_DIM), F32))
    scratch_bytes = _nbytes((c + 8, CONV_DIM), F32) + _nbytes((c, SSM_INNER), F32)
    return pl.pallas_call(
        functools.partial(_ssd_kernel, c=c),
        grid=(b, l // c),
        in_specs=[chunk(CONV_DIM), chunk(SSM_HEADS), chunk(SSM_INNER), per_b((1, 8, CONV_DIM)),
                  per_b((1, n, SSM_INNER)), full((CONV_W, CONV_DIM)), full((1, CONV_DIM)),
                  full((1, SSM_HEADS)), full((1, SSM_HEADS)), full((1, SSM_INNER)), full((1, SSM_INNER))],
        out_specs=[chunk(SSM_INNER), per_b((1, n, SSM_INNER))],
        out_shape=[jax.ShapeDtypeStruct((b, l, SSM_INNER), BF16),
                   jax.ShapeDtypeStruct((b, n, SSM_INNER), F32)],
        scratch_shapes=[pltpu.VMEM((c + 8, CONV_DIM), F32), pltpu.VMEM((c, SSM_INNER), F32)],
        compiler_params=pltpu.CompilerParams(
            dimension_semantics=("parallel", "arbitrary"),
            vmem_limit_bytes=_vmem_limit(block_bytes, scratch_bytes)),
        name="ssd",
    )(xbc, dt, z, ctx8, h0t, conv_w, conv_b.reshape(1, -1), dt_bias.reshape(1, -1), a_log.reshape(1, -1),
      jnp.repeat(d_skip, SSM_HEAD_DIM).reshape(1, -1), ssm_norm.reshape(1, -1))


def _sb_attn_kernel(*refs, tq, tkp, n_past, hps, scale):
    if n_past:
        q_ref, kn_ref, vn_ref, kp_ref, vp_ref, o_ref, acc_ref, run_ref = refs
    else:
        q_ref, kn_ref, vn_ref, o_ref, acc_ref, run_ref, w_ref = refs
    qi = pl.program_id(2)
    width = hps * SB_HEAD_DIM
    m = hps * tq
    q = (q_ref[0].astype(F32) * scale).astype(BF16)
    head_of_lane = lax.broadcasted_iota(jnp.int32, (tq, width), 1) // SB_HEAD_DIM
    zero = jnp.zeros_like(q)
    q_stack = jnp.concatenate([jnp.where(head_of_lane == h, q, zero) for h in range(hps)], axis=0)

    def suffix_ones(t):
        r = lax.broadcasted_iota(jnp.int32, (t, t), 0)
        s = lax.broadcasted_iota(jnp.int32, (t, t), 1)
        return jnp.where(r > s, -1.0, 0.0).astype(BF16)

    acc_ref[...] = jnp.zeros_like(acc_ref)
    run_ref[...] = jnp.zeros_like(run_ref)
    sign_bit = jnp.int32(-2 ** 31)

    def apply(w, v):
        acc_ref[...] += jnp.dot(w, v, preferred_element_type=F32)

    def weights(k, mask, due=None):
        ck = k.shape[0]
        sub = min(ck, SB_SUB)
        ones = suffix_ones(sub)
        zs = lax.dot_general(q_stack, k, NT_DIMS, preferred_element_type=F32)
        neg_abs = lax.bitcast_convert_type(lax.bitcast_convert_type(zs, jnp.int32) | sign_bit, F32)
        drop = jnp.maximum(zs, 0.0) + jnp.log(1.0 + jnp.exp(neg_abs))
        if mask is not None:
            drop = jnp.where(mask, drop, 0.0)
        drop_b = drop.astype(BF16)
        log_beta = zs - drop
        if due is not None:
            apply(*due)
        run = run_ref[...]
        ws = []
        for j in reversed(range(ck // sub)):
            sl = slice(j * sub, (j + 1) * sub)
            suffix = jnp.dot(drop_b[:, sl], ones, preferred_element_type=F32)
            w = jnp.exp(log_beta[:, sl] + suffix + run)
            if mask is not None:
                w = jnp.where(mask[:, sl], w, 0.0)
            ws.insert(0, w.astype(BF16))
            first = drop_b[:, j * sub:j * sub + 1].astype(F32)
            run = run + (suffix[:, 0:1] - first)
        run_ref[...] = run
        return ws[0] if len(ws) == 1 else jnp.concatenate(ws, axis=1)

    def new_chunk(ref, c):
        return ref[0, pl.ds(pl.multiple_of(c * tq, tq), tq), :]

    r = lax.rem(lax.broadcasted_iota(jnp.int32, (m, tq), 0), tq)
    s = lax.broadcasted_iota(jnp.int32, (m, tq), 1)
    w_diag = weights(new_chunk(kn_ref, qi), s < r)

    if n_past:
        apply(w_diag, new_chunk(vn_ref, qi))

        def new_body(i, carry):
            apply(weights(new_chunk(kn_ref, qi - 1 - i), None), new_chunk(vn_ref, qi - 1 - i))
            return carry

        lax.fori_loop(0, qi, new_body, 0)

        def past_body(i, carry):
            off = pl.multiple_of((n_past - 1 - i) * tkp, tkp)
            apply(weights(kp_ref[0, pl.ds(off, tkp), :].astype(BF16), None),
                  vp_ref[0, pl.ds(off, tkp), :].astype(BF16))
            return carry

        lax.fori_loop(0, n_past, past_body, 0)
    else:
        w_ref[...] = w_diag

        def new_body(i, carry):
            due = (w_ref[...], new_chunk(vn_ref, qi - i))
            w_ref[...] = weights(new_chunk(kn_ref, qi - 1 - i), None, due)
            return carry

        lax.fori_loop(0, qi, new_body, 0)
        apply(w_ref[...], new_chunk(vn_ref, 0))

    out = acc_ref[0:tq, :]
    for h in range(1, hps):
        out = jnp.where(head_of_lane == h, acc_ref[h * tq:(h + 1) * tq, :], out)
    o_ref[0] = out.astype(o_ref.dtype)


def _sb_attn(q, k_new, v_new, k_past, v_past, tq, tkp, hps):
    b, l, _ = q.shape
    assert l % tq == 0 and SB_HEADS % hps == 0
    width = hps * SB_HEAD_DIM
    m = hps * tq
    n_past = 0
    tile = pl.BlockSpec((1, tq, width), lambda i, hq, qi: (i, qi, hq))
    whole = lambda length: pl.BlockSpec((1, length, width), lambda i, hq, qi: (i, 0, hq))
    in_specs = [tile, whole(l), whole(l)]
    args = [q, k_new, v_new]
    block_bytes = 2 * _nbytes((tq, width), BF16) + 2 * _nbytes((l, width), BF16)
    if k_past is not None:
        p = k_past.shape[1]
        assert p % tkp == 0
        n_past = p // tkp
        in_specs += [whole(p), whole(p)]
        args += [k_past, v_past]
        block_bytes += 2 * _nbytes((p, width), F32)
    scratch = [pltpu.VMEM((m, width), F32), pltpu.VMEM((m, 1), F32)]
    scratch_bytes = _nbytes((m, width), F32) + _nbytes((m, LANES_V7X), F32)
    if not n_past:
        scratch.append(pltpu.VMEM((m, tq), BF16))
        scratch_bytes += _nbytes((m, tq), BF16)
    return pl.pallas_call(
        functools.partial(_sb_attn_kernel, tq=tq, tkp=tkp, n_past=n_past, hps=hps, scale=SB_HEAD_DIM ** -0.5),
        grid=(b, SB_HEADS // hps, l // tq),
        in_specs=in_specs,
        out_specs=tile,
        out_shape=jax.ShapeDtypeStruct((b, l, SB_INNER), BF16),
        scratch_shapes=scratch,
        compiler_params=pltpu.CompilerParams(
            dimension_semantics=("parallel", "parallel", "arbitrary"),
            vmem_limit_bytes=_vmem_limit(block_bytes, scratch_bytes)),
        name="sb_attn",
    )(*args)


def _merge_mem_kernel(x_ref, yn_ref, ysb_ref, gs_ref, gb_ref, mk_ref, mv_ref, wssm_ref, wsb_ref, wout_ref,
                      nq_ref, wq_ref, wo_ref, o_ref, *, scale):
    y_ssm = jnp.dot(yn_ref[0], wssm_ref[...], preferred_element_type=F32)
    y_sb = jnp.dot(ysb_ref[0], wsb_ref[...], preferred_element_type=F32)
    merged = _sigmoid(gs_ref[0]) * y_ssm + _sigmoid(gb_ref[0]) * y_sb
    x1 = x_ref[0] + jnp.dot(merged.astype(BF16), wout_ref[...], preferred_element_type=F32)
    hq = ((x1 * _rms_scale(x1)) * nq_ref[...]).astype(BF16)
    q = jnp.dot(hq, wq_ref[...], preferred_element_type=F32)
    outs = []
    for h in range(MEM_HEADS):
        sl = slice(h * MEM_HEAD_DIM, (h + 1) * MEM_HEAD_DIM)
        mk = mk_ref[0, :, sl].astype(BF16)
        mv = mv_ref[0, :, sl].astype(BF16)
        s = lax.dot_general(q[:, sl].astype(BF16), mk, NT_DIMS, preferred_element_type=F32) * scale
        e = jnp.exp(s - jnp.max(s, axis=-1, keepdims=True))
        p = e / jnp.sum(e, axis=-1, keepdims=True)
        outs.append(jnp.dot(p.astype(BF16), mv, preferred_element_type=F32))
    o = jnp.concatenate(outs, axis=1)
    o_ref[0] = x1 + jnp.dot(o.astype(BF16), wo_ref[...], preferred_element_type=F32)


def _merge_mem(x, yn, ysb, gs, gb, mk, mv, w_ssm, w_sb, w_out, norm_q, w_q, w_o, tm):
    b, l, d = x.shape
    m = mk.shape[1]
    assert l % tm == 0
    tok = lambda w: pl.BlockSpec((1, tm, w), lambda i, j: (i, j, 0))
    per_b = pl.BlockSpec((1, m, d), lambda i, j: (i, 0, 0))
    full = lambda a: pl.BlockSpec(a.shape, lambda i, j: (0, 0))
    nq = norm_q.reshape(1, d)
    weights = (w_ssm, w_sb, w_out, nq, w_q, w_o)
    block_bytes = (4 * _nbytes((tm, d), F32) + _nbytes((tm, SSM_INNER + SB_INNER), BF16)
                   + 2 * _nbytes((m, d), F32) + sum(_nbytes(w.shape, w.dtype) for w in weights))
    return pl.pallas_call(
        functools.partial(_merge_mem_kernel, scale=MEM_HEAD_DIM ** -0.5),
        grid=(b, l // tm),
        in_specs=[tok(d), tok(SSM_INNER), tok(SB_INNER), tok(d), tok(d), per_b, per_b] + [full(w) for w in weights],
        out_specs=tok(d),
        out_shape=jax.ShapeDtypeStruct((b, l, d), F32),
        compiler_params=pltpu.CompilerParams(
            dimension_semantics=("parallel", "parallel"), vmem_limit_bytes=_vmem_limit(block_bytes)),
        name="merge_mem",
    )(x, yn, ysb, gs, gb, mk, mv, *weights)


def _route(logits):
    lane = lax.broadcasted_iota(jnp.int32, logits.shape, 1)
    big = jnp.int32(LANES_V7X)
    neg = -jnp.inf
    is_group = lane < MOE_GROUPS
    gl = jnp.where(is_group, logits, neg)
    ge = jnp.where(is_group, jnp.exp(gl - jnp.max(gl, axis=-1, keepdims=True)), 0.0)
    pg = ge / jnp.sum(ge, axis=-1, keepdims=True)
    p_sel = jnp.max(pg, axis=-1, keepdims=True)
    g_sel = jnp.min(jnp.where(is_group & (pg == p_sel), lane, big), axis=-1, keepdims=True)
    e_idx = lane - MOE_GROUPS
    in_group = (e_idx >= 0) & (e_idx < N_EXPERTS) & ((e_idx // EXPERTS_PER_GROUP) == g_sel)
    le = jnp.where(in_group, logits, neg)
    v1 = jnp.max(le, axis=-1, keepdims=True)
    i1 = jnp.min(jnp.where(in_group & (le == v1), lane, big), axis=-1, keepdims=True)
    rest = in_group & (lane != i1)
    le2 = jnp.where(rest, logits, neg)
    v2 = jnp.max(le2, axis=-1, keepdims=True)
    i2 = jnp.min(jnp.where(rest & (le2 == v2), lane, big), axis=-1, keepdims=True)
    e2 = jnp.exp(v2 - v1)
    den = 1.0 + e2
    dense_w = (jnp.where(lane == i1, (1.0 / den) * p_sel, 0.0)
               + jnp.where(lane == i2, (e2 / den) * p_sel, 0.0))
    return dense_w, g_sel


def _moe_hidden(x_ref, nf_ref):
    x = x_ref[...]
    return ((x * _rms_scale(x)) * nf_ref[...]).astype(BF16)


def _moe_rank_kernel(x_ref, nf_ref, wr_ref, info_ref, cnt_ref, carry_ref):
    i = pl.program_id(0)

    @pl.when(i == 0)
    def _():
        carry_ref[...] = jnp.zeros_like(carry_ref)

    hb = _moe_hidden(x_ref, nf_ref)
    _, g_sel = _route(jnp.dot(hb, wr_ref[...], preferred_element_type=F32))
    tm = hb.shape[0]
    lane = lax.broadcasted_iota(jnp.int32, (tm, LANES_V7X), 1)
    onehot = lane == g_sel
    r = lax.broadcasted_iota(jnp.int32, (tm, tm), 0)
    c = lax.broadcasted_iota(jnp.int32, (tm, tm), 1)
    before = jnp.dot((c < r).astype(BF16), onehot.astype(BF16), preferred_element_type=F32) + carry_ref[...]
    rank = jnp.sum(jnp.where(onehot, before, 0.0), axis=-1, keepdims=True)
    info = jnp.where(lane == 0, g_sel.astype(F32), jnp.where(lane == 1, rank, 0.0))
    info_ref[...] = info.astype(jnp.int32)
    carry_ref[...] += jnp.sum(onehot.astype(F32), axis=0, keepdims=True)
    cnt_ref[...] = carry_ref[...].astype(jnp.int32)


def _moe_experts_kernel(tile_ref, exp_ref, first_ref, last_ref, act_ref, x_ref, nf_ref, wr_ref, wg_ref, wu_ref,
                        wd_ref, nfin_ref, y_ref, h_ref, dw_ref, acc_ref, *, final_norm):
    w = pl.program_id(0)

    @pl.when(first_ref[w] == 1)
    def _():
        hb = _moe_hidden(x_ref, nf_ref)
        h_ref[...] = hb
        dw_ref[...] = _route(jnp.dot(hb, wr_ref[...], preferred_element_type=F32))[0]
        acc_ref[...] = jnp.zeros_like(acc_ref)

    @pl.when(act_ref[w] == 1)
    def _():
        hb = h_ref[...]
        gate = jnp.dot(hb, wg_ref[0], preferred_element_type=F32)
        up = jnp.dot(hb, wu_ref[0], preferred_element_type=F32)
        he = (gate * _sigmoid(gate)) * up
        down = jnp.dot(he.astype(BF16), wd_ref[0], preferred_element_type=F32)
        lane = lax.broadcasted_iota(jnp.int32, dw_ref.shape, 1)
        dcol = jnp.sum(jnp.where(lane == exp_ref[w] + MOE_GROUPS, dw_ref[...], 0.0), axis=-1, keepdims=True)
        acc_ref[...] += dcol * down

    @pl.when(last_ref[w] == 1)
    def _():
        xo = x_ref[...] + acc_ref[...]
        y_ref[...] = (xo * _rms_scale(xo)) * nfin_ref[...] if final_norm else xo


def _permute_rows_kernel(idx_ref, src_ref, dst_ref, sem, *, rows, scatter):
    def row_copy(r, k):
        src_row, dst_row = (r, k) if scatter else (k, r)
        return pltpu.make_async_copy(src_ref.at[pl.ds(src_row, 1)], dst_ref.at[pl.ds(dst_row, 1)], sem.at[0])

    def start(r, carry):
        row_copy(r, idx_ref[0, 0, r]).start()
        return carry

    def wait(r, carry):
        row_copy(0, 0).wait()
        return carry

    lax.fori_loop(0, rows, start, 0, unroll=8)
    lax.fori_loop(0, rows, wait, 0, unroll=8)


def _permute_rows(src, idx, scatter, name):
    t, d = src.shape
    rows = _tile(t, PERMUTE_ROWS)
    tile = pl.BlockSpec((rows, d), lambda s: (s, 0))
    hbm = pl.BlockSpec(memory_space=pl.ANY)
    return pl.pallas_call(
        functools.partial(_permute_rows_kernel, rows=rows, scatter=scatter),
        grid=(t // rows,),
        in_specs=[pl.BlockSpec((1, 1, rows), lambda s: (s, 0, 0), memory_space=pltpu.SMEM),
                  tile if scatter else hbm],
        out_specs=hbm if scatter else tile,
        out_shape=jax.ShapeDtypeStruct((t, d), src.dtype),
        scratch_shapes=[pltpu.SemaphoreType.DMA((1,))],
        compiler_params=pltpu.CompilerParams(
            dimension_semantics=("arbitrary",), vmem_limit_bytes=_vmem_limit(_nbytes((rows, d), src.dtype))),
        name=name,
    )(idx.reshape(t // rows, 1, rows), src)


def _moe_plan(gid, rank, counts, n_tiles, tm):
    i32 = jnp.int32
    ends = jnp.cumsum(counts).astype(i32)
    pos = (ends - counts)[gid] + rank
    first_row = jnp.arange(n_tiles, dtype=i32) * tm
    groups_before = lambda row: jnp.sum((ends[None, :] <= row[:, None]).astype(i32), axis=1)
    g_lo, g_hi = groups_before(first_row), groups_before(first_row + (tm - 1))
    n_items = EXPERTS_PER_GROUP * (g_hi - g_lo + 1)
    item_end = jnp.cumsum(n_items).astype(i32)
    n_work = EXPERTS_PER_GROUP * (n_tiles + MOE_GROUPS - 1)
    w = jnp.arange(n_work, dtype=i32)
    tile = jnp.minimum(jnp.sum((item_end[None, :] <= w[:, None]).astype(i32), axis=1), n_tiles - 1)
    local = w - (item_end - n_items)[tile]
    active = w < item_end[-1]
    expert = jnp.where(active, EXPERTS_PER_GROUP * g_lo[tile] + local,
                       EXPERTS_PER_GROUP * g_hi[-1] + EXPERTS_PER_GROUP - 1)
    first = active & (local == 0)
    last = active & (local == n_items[tile] - 1)
    return pos.astype(i32), (tile, expert.astype(i32), first.astype(i32), last.astype(i32), active.astype(i32))


def _moe(x2d, norm_ffn, w_router, w_gate, w_up, w_down, norm_final, final_norm, tm):
    t, d = x2d.shape
    assert t % tm == 0 and t < 2 ** 24
    n_tiles = t // tm
    n_e, _, ff = w_gate.shape
    nf, nfin = norm_ffn.reshape(1, d), norm_final.reshape(1, d)
    const = lambda a: pl.BlockSpec(a.shape, lambda i, *_: (0, 0))

    rank_bytes = _nbytes((tm, d), F32) + _nbytes(w_router.shape, BF16) + _nbytes((tm, LANES_V7X), F32)
    info, cnt = pl.pallas_call(
        _moe_rank_kernel,
        grid=(n_tiles,),
        in_specs=[pl.BlockSpec((tm, d), lambda i: (i, 0)), const(nf), const(w_router)],
        out_specs=[pl.BlockSpec((tm, LANES_V7X), lambda i: (i, 0)), pl.BlockSpec((1, LANES_V7X), lambda i: (0, 0))],
        out_shape=[jax.ShapeDtypeStruct((t, LANES_V7X), jnp.int32), jax.ShapeDtypeStruct((1, LANES_V7X), jnp.int32)],
        scratch_shapes=[pltpu.VMEM((1, LANES_V7X), F32)],
        compiler_params=pltpu.CompilerParams(
            dimension_semantics=("arbitrary",), vmem_limit_bytes=_vmem_limit(rank_bytes)),
        name="moe_rank",
    )(x2d, nf, w_router)
    pos, plan = _moe_plan(info[:, 0], info[:, 1], cnt[0, :MOE_GROUPS], n_tiles, tm)

    xs = _permute_rows(x2d, pos, True, "moe_sort")
    by_tile = lambda wi, tile, *_: (tile[wi], 0)
    by_expert = lambda wi, tile, exp, *_: (exp[wi], 0, 0)
    block_bytes = (2 * _nbytes((tm, d), F32) + 2 * _nbytes((1, d), F32) + _nbytes(w_router.shape, BF16)
                   + 3 * _nbytes((d, ff), BF16))
    scratch_bytes = _nbytes((tm, d), BF16) + _nbytes((tm, LANES_V7X), F32) + _nbytes((tm, d), F32)
    ys = pl.pallas_call(
        functools.partial(_moe_experts_kernel, final_norm=final_norm),
        grid_spec=pltpu.PrefetchScalarGridSpec(
            num_scalar_prefetch=len(plan),
            grid=(plan[0].shape[0],),
            in_specs=[pl.BlockSpec((tm, d), by_tile), const(nf), const(w_router),
                      pl.BlockSpec((1, d, ff), by_expert), pl.BlockSpec((1, d, ff), by_expert),
                      pl.BlockSpec((1, ff, d), by_expert), const(nfin)],
            out_specs=pl.BlockSpec((tm, d), by_tile),
            scratch_shapes=[pltpu.VMEM((tm, d), BF16), pltpu.VMEM((tm, LANES_V7X), F32), pltpu.VMEM((tm, d), F32)]),
        out_shape=jax.ShapeDtypeStruct((t, d), F32),
        compiler_params=pltpu.CompilerParams(
            dimension_semantics=("arbitrary",), vmem_limit_bytes=_vmem_limit(block_bytes, scratch_bytes)),
        name="moe_experts",
    )(*plan, xs, nf, w_router, w_gate, w_up, w_down, nfin)
    return _permute_rows(ys, pos, False, "moe_unsort")


def _tile(n, pref):
    t = min(n, pref)
    while n % t:
        t //= 2
    return t


def _split_w_in(w_in):
    offs = [0]
    for s in IN_SIZES:
        offs.append(offs[-1] + s)
    return [w_in[:, offs[i]:offs[i + 1]].astype(BF16) for i in range(len(IN_SIZES))]


def _group_step(x, conv_ctx, ssm_h0, past_k, past_v, mem_k, mem_v, lw):
    b, l, d = x.shape
    t = b * l
    x2d = x.reshape(t, d)
    w_z, w_xbc, w_dt, w_q, w_k, w_v, w_gs, w_gb = lw["w_in_parts"]
    z, xbc, dt, q, k, kb, v, vb, gs, gb = _norm_proj(
        x2d, lw["norm_mix"], [w_z, w_xbc, w_dt, w_q, w_k, w_v, w_gs, w_gb],
        [(F32,), (F32,), (F32,), (BF16,), (F32, BF16), (F32, BF16), (F32,), (F32,)],
        _tile(t, 256), "in_proj")
    r3 = lambda a: a.reshape(b, l, a.shape[-1])

    h0t = ssm_h0.transpose(0, 3, 1, 2).reshape(b, SSM_STATE, SSM_INNER)
    ctx8 = jnp.pad(conv_ctx, ((0, 0), (SUBLANES_V7X - (CONV_W - 1), 0), (0, 0)))
    yn, ht = _ssd(r3(xbc), r3(dt), r3(z), ctx8, h0t, lw["conv_w"], lw["conv_b"], lw["dt_bias"], lw["a_log"],
                  lw["d_skip"], lw["ssm_norm"], _tile(l, 128))
    ssm_new = ht.reshape(b, SSM_STATE, SSM_HEADS, SSM_HEAD_DIM).transpose(0, 2, 3, 1)
    assert l >= CONV_W - 1
    conv_new = r3(xbc)[:, l - (CONV_W - 1):, :]

    tq = _tile(l, SB_CHUNK)
    hps = min(SB_HEADS, SB_STACK_ROWS // tq)
    ysb = _sb_attn(r3(q), r3(kb), r3(vb), past_k, past_v, tq, SB_CHUNK, hps)

    x_mem = _merge_mem(x, yn, ysb, r3(gs), r3(gb), mem_k, mem_v, lw["w_ssm_br"], lw["w_sb_br"], lw["w_out"],
                       lw["norm_mem_q"], lw["w_mem_q"], lw["w_mem_o"], _tile(l, 512))
    kh = k.reshape(b, l, SB_HEADS, SB_HEAD_DIM)
    vh = v.reshape(b, l, SB_HEADS, SB_HEAD_DIM)
    return x_mem, conv_new, ssm_new, kh, vh


def kernel(x_prompt, x_sample, cache_sb_k, cache_sb_v, state_ssm, state_conv, cache_mem_k, cache_mem_v, mem_prompt, norm_mix, w_in, conv_w, conv_b, dt_bias, a_log, d_skip, ssm_norm, w_ssm_br, w_sb_br, w_out, norm_mem_q, norm_mem_kv, w_mem_q, w_mem_kv, w_mem_o, norm_ffn, w_group_router, w_expert_router, w_gate_e, w_up_e, w_down_e, norm_final):
    depth = norm_mix.shape[0]
    bp, lp, d = x_prompt.shape
    bs, ls, _ = x_sample.shape
    xp, xs = x_prompt, x_sample
    outs = {name: [] for name in ("skp", "svp", "ssp", "scp", "mkp", "mvp", "sks", "svs", "sss", "scs")}
    for l in range(depth):
        router = jnp.concatenate([w_group_router[l], w_expert_router[l]], axis=1)
        router = jnp.pad(router, ((0, 0), (0, LANES_V7X - router.shape[1]))).astype(BF16)
        lw = dict(
            norm_mix=norm_mix[l], w_in_parts=_split_w_in(w_in[l]), conv_w=conv_w[l], conv_b=conv_b[l],
            dt_bias=dt_bias[l], a_log=a_log[l], d_skip=d_skip[l], ssm_norm=ssm_norm[l],
            w_ssm_br=w_ssm_br[l].astype(BF16), w_sb_br=w_sb_br[l].astype(BF16), w_out=w_out[l].astype(BF16),
            norm_mem_q=norm_mem_q[l], w_mem_q=w_mem_q[l].astype(BF16), w_mem_o=w_mem_o[l].astype(BF16))
        moe_w = (norm_ffn[l], router, w_gate_e[l].astype(BF16), w_up_e[l].astype(BF16), w_down_e[l].astype(BF16))
        last = l == depth - 1

        m = mem_prompt.shape[1]
        w_kv = w_mem_kv[l].astype(BF16)
        mk_p, mv_p = _norm_proj(mem_prompt.reshape(bp * m, d), norm_mem_kv[l], [w_kv[:, :d], w_kv[:, d:]],
                                [(F32,), (F32,)], _tile(bp * m, 256), "mem_kv")
        mk_p, mv_p = mk_p.reshape(bp, m, d), mv_p.reshape(bp, m, d)
        conv0 = jnp.zeros((bp, CONV_W - 1, CONV_DIM), xp.dtype)
        h0 = jnp.zeros((bp, SSM_HEADS, SSM_HEAD_DIM, SSM_STATE), xp.dtype)
        xp, conv_p, ssm_p, k_p, v_p = _group_step(xp, conv0, h0, None, None, mk_p, mv_p, lw)
        xp = _moe(xp.reshape(bp * lp, d), *moe_w, norm_final, last, _tile(bp * lp, MOE_TILE)).reshape(bp, lp, d)
        outs["skp"].append(k_p); outs["svp"].append(v_p); outs["ssp"].append(ssm_p); outs["scp"].append(conv_p)
        outs["mkp"].append(mk_p.reshape(bp, m, MEM_HEADS, MEM_HEAD_DIM))
        outs["mvp"].append(mv_p.reshape(bp, m, MEM_HEADS, MEM_HEAD_DIM))

        past = cache_sb_k.shape[2]
        ms = cache_mem_k.shape[2]
        xs, conv_s, ssm_s, k_s, v_s = _group_step(
            xs, state_conv[l], state_ssm[l], cache_sb_k[l].reshape(bs, past, SB_INNER),
            cache_sb_v[l].reshape(bs, past, SB_INNER), cache_mem_k[l].reshape(bs, ms, d),
            cache_mem_v[l].reshape(bs, ms, d), lw)
        xs = _moe(xs.reshape(bs * ls, d), *moe_w, norm_final, last, _tile(bs * ls, MOE_TILE)).reshape(bs, ls, d)
        outs["sks"].append(k_s); outs["svs"].append(v_s); outs["sss"].append(ssm_s); outs["scs"].append(conv_s)

    st = lambda name: jnp.stack(outs[name])
    return (xp, xs, st("skp"), st("svp"), st("ssp"), st("scp"), st("mkp"), st("mvp"),
            st("sks"), st("svs"), st("sss"), st("scs"))
```

```python
import functools

import jax
import jax.numpy as jnp
from jax import lax
from jax.experimental import pallas as pl
from jax.experimental.pallas import tpu as pltpu

F32 = jnp.float32
BF16 = jnp.bfloat16
EPS = 1e-6

D_MODEL = 1024
SSM_HEADS = 16
SSM_HEAD_DIM = 64
SSM_INNER = SSM_HEADS * SSM_HEAD_DIM
SSM_GROUPS = 2
SSM_STATE = 128
CONV_W = 4
CONV_DIM = SSM_INNER + 2 * SSM_GROUPS * SSM_STATE
SB_HEADS = 8
SB_HEAD_DIM = 64
SB_INNER = SB_HEADS * SB_HEAD_DIM
MEM_HEADS = 4
MEM_HEAD_DIM = D_MODEL // MEM_HEADS
MOE_GROUPS = 4
EXPERTS_PER_GROUP = 4
N_EXPERTS = MOE_GROUPS * EXPERTS_PER_GROUP
IN_SIZES = (SSM_INNER, CONV_DIM, SSM_HEADS, SB_INNER, SB_INNER, SB_INNER, D_MODEL, D_MODEL)

LANES_V7X = 128
SUBLANES_V7X = 8
VMEM_BYTES_V7X = 64 * 1024 * 1024
COMPILER_TEMP_BYTES = 20 * 1024 * 1024
VMEM_CAP_BYTES = 60000 * 1024

NT_DIMS = (((1,), (1,)), ((), ()))

SB_CHUNK = 512
SB_SUB = 256
SB_STACK_ROWS = 2048

MOE_TILE = 512
PERMUTE_ROWS = 2048


def _vmem_limit(block_bytes, scratch_bytes=0):
    need = 2 * block_bytes + scratch_bytes + COMPILER_TEMP_BYTES
    return int(min(need, VMEM_CAP_BYTES))


def _nbytes(shape, dtype):
    n = 1
    for s in shape:
        n *= s
    return n * jnp.dtype(dtype).itemsize


def _sigmoid(x):
    return 1.0 / (1.0 + jnp.exp(-x))


def _softplus(x):
    return jnp.maximum(x, 0.0) + jnp.log1p(jnp.exp(-jnp.abs(x)))


def _rms_scale(x):
    return lax.rsqrt(jnp.mean(x * x, axis=-1, keepdims=True) + EPS)


def _norm_proj_kernel(x_ref, g_ref, *refs, out_plan):
    n_w = len(out_plan)
    w_refs = refs[:n_w]
    o_refs = refs[n_w:]
    x = x_ref[...]
    h = ((x * _rms_scale(x)) * g_ref[...]).astype(BF16)
    oi = 0
    for w_ref, outs in zip(w_refs, out_plan):
        y = jnp.dot(h, w_ref[...], preferred_element_type=F32)
        for dt, _ in outs:
            o_ref = o_refs[oi]
            o_ref[...] = y.astype(dt).reshape(o_ref.shape)
            oi += 1


def _norm_proj(x2d, gain, weights, out_plan, tm, name):
    t, d = x2d.shape
    assert t % tm == 0
    in_specs = [pl.BlockSpec((tm, d), lambda i: (i, 0)), pl.BlockSpec((1, d), lambda i: (0, 0))]
    block_bytes = _nbytes((tm, d), F32) + _nbytes((1, d), F32)
    for w in weights:
        in_specs.append(pl.BlockSpec(w.shape, lambda i: (0, 0)))
        block_bytes += _nbytes(w.shape, w.dtype)
    out_shapes, out_specs = [], []
    for w, outs in zip(weights, out_plan):
        n = w.shape[1]
        for dt, heads in outs:
            if heads is None:
                out_shapes.append(jax.ShapeDtypeStruct((t, n), dt))
                out_specs.append(pl.BlockSpec((tm, n), lambda i: (i, 0)))
                block_bytes += _nbytes((tm, max(n, LANES_V7X)), dt)
            else:
                out_shapes.append(jax.ShapeDtypeStruct((t, heads, n // heads), dt))
                out_specs.append(pl.BlockSpec((tm, heads, n // heads), lambda i: (i, 0, 0)))
                block_bytes += _nbytes((tm, heads, max(n // heads, LANES_V7X)), dt)
    return pl.pallas_call(
        functools.partial(_norm_proj_kernel, out_plan=tuple(out_plan)),
        grid=(t // tm,),
        in_specs=in_specs,
        out_specs=out_specs,
        out_shape=out_shapes,
        compiler_params=pltpu.CompilerParams(
            dimension_semantics=("parallel",), vmem_limit_bytes=_vmem_limit(block_bytes)),
        name=name,
    )(x2d, gain.reshape(1, d), *weights)


def _ssd_kernel(xbc_ref, dt_ref, z_ref, ctx_ref, h0_ref, convw_ref, convb_ref, dtb_ref, alog_ref,
                dskip_ref, norm_ref, yn_ref, ht_ref, ext_ref, y_ref, *, c):
    j = pl.program_id(1)
    halo = SUBLANES_V7X
    n = SSM_STATE
    pair_w = 2 * SSM_HEAD_DIM

    @pl.when(j == 0)
    def _():
        ext_ref[0:halo, :] = ctx_ref[0]
        ht_ref[0] = h0_ref[0]

    raw = xbc_ref[0]
    ext_ref[halo:halo + c, :] = raw
    ext = ext_ref[...]
    conv = convb_ref[...]
    for tap in range(CONV_W - 1):
        shifted = pltpu.roll(ext, CONV_W - 1 - tap, axis=0)[halo:halo + c, :]
        conv = conv + shifted * convw_ref[tap:tap + 1, :]
    conv = conv + raw * convw_ref[CONV_W - 1:CONV_W, :]
    ext_ref[0:halo, :] = ext_ref[c:c + halo, :]
    act = conv * _sigmoid(conv)

    dt = _softplus(dt_ref[0] + dtb_ref[...])
    a = -jnp.exp(alog_ref[...])
    row = lax.broadcasted_iota(jnp.int32, (c, c), 0)
    col = lax.broadcasted_iota(jnp.int32, (c, c), 1)
    tril = row >= col
    cum = jnp.dot(tril.astype(F32), dt * a, precision=lax.Precision.HIGHEST,
                  preferred_element_type=F32)
    hrow = lax.broadcasted_iota(jnp.int32, (SSM_HEADS, SSM_HEADS), 0)
    hcol = lax.broadcasted_iota(jnp.int32, (SSM_HEADS, SSM_HEADS), 1)
    eye_h = (hrow == hcol).astype(F32)
    cum_t = lax.dot_general(eye_h, cum, NT_DIMS, precision=lax.Precision.HIGHEST,
                            preferred_element_type=F32)
    cum_last = cum[c - 1:c, :]
    exp_cum = jnp.exp(cum)
    dt_end = dt * jnp.exp(cum_last - cum)
    chunk_decay = jnp.exp(cum_last)

    nrow = lax.broadcasted_iota(jnp.int32, (n, n), 0)
    ncol = lax.broadcasted_iota(jnp.int32, (n, n), 1)
    eye_n = (nrow == ncol).astype(BF16)
    lane_c = lax.broadcasted_iota(jnp.int32, (c, pair_w), 1) < SSM_HEAD_DIM
    lane_n = lax.broadcasted_iota(jnp.int32, (n, pair_w), 1) < SSM_HEAD_DIM
    lane_1 = lax.broadcasted_iota(jnp.int32, (1, pair_w), 1) < SSM_HEAD_DIM

    heads_per_group = SSM_HEADS // SSM_GROUPS
    for g in range(SSM_GROUPS):
        b_lo = SSM_INNER + g * n
        c_lo = SSM_INNER + SSM_GROUPS * n + g * n
        bm = act[:, b_lo:b_lo + n].astype(BF16)
        cm = act[:, c_lo:c_lo + n]
        cb = lax.dot_general(cm.astype(BF16), bm, NT_DIMS, preferred_element_type=F32)
        bm_t = lax.dot_general(eye_n, bm, NT_DIMS, preferred_element_type=F32).astype(BF16)
        for pair in range(heads_per_group // 2):
            ha = g * heads_per_group + 2 * pair
            hb = ha + 1
            lo = ha * SSM_HEAD_DIM
            x_pair = act[:, lo:lo + pair_w]
            st = ht_ref[0, :, lo:lo + pair_w]
            dt_pair = jnp.where(lane_c, dt[:, ha:ha + 1], dt[:, hb:hb + 1])
            dte_pair = jnp.where(lane_c, dt_end[:, ha:ha + 1], dt_end[:, hb:hb + 1])
            rhs = jnp.concatenate([(x_pair * dt_pair).astype(BF16), st.astype(BF16)], axis=0)
            ys = []
            for h in (ha, hb):
                seg = cum[:, h:h + 1] - cum_t[h:h + 1, :]
                decay = jnp.exp(jnp.where(tril, seg, -jnp.inf))
                lhs = jnp.concatenate(
                    [(cb * decay).astype(BF16), (cm * exp_cum[:, h:h + 1]).astype(BF16)], axis=1)
                ys.append(jnp.dot(lhs, rhs, preferred_element_type=F32))
            y_ref[:, lo:lo + pair_w] = jnp.where(lane_c, ys[0], ys[1])
            upd = jnp.dot(bm_t, (x_pair * dte_pair).astype(BF16), preferred_element_type=F32)
            cd_pair = jnp.where(lane_1, chunk_decay[:, ha:ha + 1], chunk_decay[:, hb:hb + 1])
            ht_ref[0, :, lo:lo + pair_w] = st * cd_pair + upd

    y = y_ref[...] + dskip_ref[...] * act[:, :SSM_INNER]
    zz = z_ref[0]
    gated = y * (zz * _sigmoid(zz))
    yn_ref[0] = ((gated * _rms_scale(gated)) * norm_ref[...]).astype(yn_ref.dtype)


def _ssd(xbc, dt, z, ctx8, h0t, conv_w, conv_b, dt_bias, a_log, d_skip, ssm_norm, c):
    b, l, _ = xbc.shape
    assert l % c == 0 and c % SUBLANES_V7X == 0
    n = SSM_STATE
    per_b = lambda shape: pl.BlockSpec(shape, lambda i, j: (i, 0, 0))
    chunk = lambda w: pl.BlockSpec((1, c, w), lambda i, j: (i, j, 0))
    full = lambda shape: pl.BlockSpec(shape, lambda i, j: (0, 0))
    block_bytes = (_nbytes((c, CONV_DIM), F32) + _nbytes((c, LANES_V7X), F32) + _nbytes((c, SSM_INNER), F32)
                   + _nbytes((8, CONV_DIM), F32) + 2 * _nbytes((n, SSM_INNER), F32)
                   + _nbytes((c, SSM_INNER), BF16) + 8 * _nbytes((8, CONV_DIM), F32))
    scratch_bytes = _nbytes((c + 8, CONV_DIM), F32) + _nbytes((c, SSM_INNER), F32)
    return pl.pallas_call(
        functools.partial(_ssd_kernel, c=c),
        grid=(b, l // c),
        in_specs=[chunk(CONV_DIM), chunk(SSM_HEADS), chunk(SSM_INNER), per_b((1, 8, CONV_DIM)),
                  per_b((1, n, SSM_INNER)), full((CONV_W, CONV_DIM)), full((1, CONV_DIM)),
                  full((1, SSM_HEADS)), full((1, SSM_HEADS)), full((1, SSM_INNER)), full((1, SSM_INNER))],
        out_specs=[chunk(SSM_INNER), per_b((1, n, SSM_INNER))],
        out_shape=[jax.ShapeDtypeStruct((b, l, SSM_INNER), BF16),
                   jax.ShapeDtypeStruct((b, n, SSM_INNER), F32)],
        scratch_shapes=[pltpu.VMEM((c + 8, CONV_DIM), F32), pltpu.VMEM((c, SSM_INNER), F32)],
        compiler_params=pltpu.CompilerParams(
            dimension_semantics=("parallel", "arbitrary"),
            vmem_limit_bytes=_vmem_limit(block_bytes, scratch_bytes)),
        name="ssd",
    )(xbc, dt, z, ctx8, h0t, conv_w, conv_b.reshape(1, -1), dt_bias.reshape(1, -1), a_log.reshape(1, -1),
      jnp.repeat(d_skip, SSM_HEAD_DIM).reshape(1, -1), ssm_norm.reshape(1, -1))


def _sb_attn_kernel(*refs, tq, tkp, n_past, hps, scale):
    if n_past:
        q_ref, kn_ref, vn_ref, kp_ref, vp_ref, o_ref, acc_ref, run_ref = refs
    else:
        q_ref, kn_ref, vn_ref, o_ref, acc_ref, run_ref, w_ref = refs
    qi = pl.program_id(2)
    width = hps * SB_HEAD_DIM
    m = hps * tq
    q = (q_ref[0].astype(F32) * scale).astype(BF16)
    head_of_lane = lax.broadcasted_iota(jnp.int32, (tq, width), 1) // SB_HEAD_DIM
    zero = jnp.zeros_like(q)
    q_stack = jnp.concatenate([jnp.where(head_of_lane == h, q, zero) for h in range(hps)], axis=0)

    def suffix_ones(t):
        r = lax.broadcasted_iota(jnp.int32, (t, t), 0)
        s = lax.broadcasted_iota(jnp.int32, (t, t), 1)
        return jnp.where(r > s, -1.0, 0.0).astype(BF16)

    acc_ref[...] = jnp.zeros_like(acc_ref)
    run_ref[...] = jnp.zeros_like(run_ref)
    sign_bit = jnp.int32(-2 ** 31)

    def apply(w, v):
        acc_ref[...] += jnp.dot(w, v, preferred_element_type=F32)

    def weights(k, mask, due=None):
        ck = k.shape[0]
        sub = min(ck, SB_SUB)
        ones = suffix_ones(sub)
        zs = lax.dot_general(q_stack, k, NT_DIMS, preferred_element_type=F32)
        neg_abs = lax.bitcast_convert_type(lax.bitcast_convert_type(zs, jnp.int32) | sign_bit, F32)
        drop = jnp.maximum(zs, 0.0) + jnp.log(1.0 + jnp.exp(neg_abs))
        if mask is not None:
            drop = jnp.where(mask, drop, 0.0)
        drop_b = drop.astype(BF16)
        log_beta = zs - drop
        if due is not None:
            apply(*due)
        run = run_ref[...]
        ws = []
        for j in reversed(range(ck // sub)):
            sl = slice(j * sub, (j + 1) * sub)
            suffix = jnp.dot(drop_b[:, sl], ones, preferred_element_type=F32)
            w = jnp.exp(log_beta[:, sl] + suffix + run)
            if mask is not None:
                w = jnp.where(mask[:, sl], w, 0.0)
            ws.insert(0, w.astype(BF16))
            first = drop_b[:, j * sub:j * sub + 1].astype(F32)
            run = run + (suffix[:, 0:1] - first)
        run_ref[...] = run
        return ws[0] if len(ws) == 1 else jnp.concatenate(ws, axis=1)

    def new_chunk(ref, c):
        return ref[0, pl.ds(pl.multiple_of(c * tq, tq), tq), :]

    r = lax.rem(lax.broadcasted_iota(jnp.int32, (m, tq), 0), tq)
    s = lax.broadcasted_iota(jnp.int32, (m, tq), 1)
    w_diag = weights(new_chunk(kn_ref, qi), s < r)

    if n_past:
        apply(w_diag, new_chunk(vn_ref, qi))

        def new_body(i, carry):
            apply(weights(new_chunk(kn_ref, qi - 1 - i), None), new_chunk(vn_ref, qi - 1 - i))
            return carry

        lax.fori_loop(0, qi, new_body, 0)

        def past_body(i, carry):
            off = pl.multiple_of((n_past - 1 - i) * tkp, tkp)
            apply(weights(kp_ref[0, pl.ds(off, tkp), :].astype(BF16), None),
                  vp_ref[0, pl.ds(off, tkp), :].astype(BF16))
            return carry

        lax.fori_loop(0, n_past, past_body, 0)
    else:
        w_ref[...] = w_diag

        def new_body(i, carry):
            due = (w_ref[...], new_chunk(vn_ref, qi - i))
            w_ref[...] = weights(new_chunk(kn_ref, qi - 1 - i), None, due)
            return carry

        lax.fori_loop(0, qi, new_body, 0)
        apply(w_ref[...], new_chunk(vn_ref, 0))

    out = acc_ref[0:tq, :]
    for h in range(1, hps):
        out = jnp.where(head_of_lane == h, acc_ref[h * tq:(h + 1) * tq, :], out)
    o_ref[0] = out.astype(o_ref.dtype)


def _sb_attn(q, k_new, v_new, k_past, v_past, tq, tkp, hps):
    b, l, _ = q.shape
    assert l % tq == 0 and SB_HEADS % hps == 0
    width = hps * SB_HEAD_DIM
    m = hps * tq
    n_past = 0
    tile = pl.BlockSpec((1, tq, width), lambda i, hq, qi: (i, qi, hq))
    whole = lambda length: pl.BlockSpec((1, length, width), lambda i, hq, qi: (i, 0, hq))
    in_specs = [tile, whole(l), whole(l)]
    args = [q, k_new, v_new]
    block_bytes = 2 * _nbytes((tq, width), BF16) + 2 * _nbytes((l, width), BF16)
    if k_past is not None:
        p = k_past.shape[1]
        assert p % tkp == 0
        n_past = p // tkp
        in_specs += [whole(p), whole(p)]
        args += [k_past, v_past]
        block_bytes += 2 * _nbytes((p, width), F32)
    scratch = [pltpu.VMEM((m, width), F32), pltpu.VMEM((m, 1), F32)]
    scratch_bytes = _nbytes((m, width), F32) + _nbytes((m, LANES_V7X), F32)
    if not n_past:
        scratch.append(pltpu.VMEM((m, tq), BF16))
        scratch_bytes += _nbytes((m, tq), BF16)
    return pl.pallas_call(
        functools.partial(_sb_attn_kernel, tq=tq, tkp=tkp, n_past=n_past, hps=hps, scale=SB_HEAD_DIM ** -0.5),
        grid=(b, SB_HEADS // hps, l // tq),
        in_specs=in_specs,
        out_specs=tile,
        out_shape=jax.ShapeDtypeStruct((b, l, SB_INNER), BF16),
        scratch_shapes=scratch,
        compiler_params=pltpu.CompilerParams(
            dimension_semantics=("parallel", "parallel", "arbitrary"),
            vmem_limit_bytes=_vmem_limit(block_bytes, scratch_bytes)),
        name="sb_attn",
    )(*args)


def _merge_mem_kernel(x_ref, yn_ref, ysb_ref, gs_ref, gb_ref, mk_ref, mv_ref, wssm_ref, wsb_ref, wout_ref,
                      nq_ref, wq_ref, wo_ref, o_ref, *, scale):
    y_ssm = jnp.dot(yn_ref[0], wssm_ref[...], preferred_element_type=F32)
    y_sb = jnp.dot(ysb_ref[0], wsb_ref[...], preferred_element_type=F32)
    merged = _sigmoid(gs_ref[0]) * y_ssm + _sigmoid(gb_ref[0]) * y_sb
    x1 = x_ref[0] + jnp.dot(merged.astype(BF16), wout_ref[...], preferred_element_type=F32)
    hq = ((x1 * _rms_scale(x1)) * nq_ref[...]).astype(BF16)
    q = jnp.dot(hq, wq_ref[...], preferred_element_type=F32)
    outs = []
    for h in range(MEM_HEADS):
        sl = slice(h * MEM_HEAD_DIM, (h + 1) * MEM_HEAD_DIM)
        mk = mk_ref[0, :, sl].astype(BF16)
        mv = mv_ref[0, :, sl].astype(BF16)
        s = lax.dot_general(q[:, sl].astype(BF16), mk, NT_DIMS, preferred_element_type=F32) * scale
        e = jnp.exp(s - jnp.max(s, axis=-1, keepdims=True))
        p = e / jnp.sum(e, axis=-1, keepdims=True)
        outs.append(jnp.dot(p.astype(BF16), mv, preferred_element_type=F32))
    o = jnp.concatenate(outs, axis=1)
    o_ref[0] = x1 + jnp.dot(o.astype(BF16), wo_ref[...], preferred_element_type=F32)


def _merge_mem(x, yn, ysb, gs, gb, mk, mv, w_ssm, w_sb, w_out, norm_q, w_q, w_o, tm):
    b, l, d = x.shape
    m = mk.shape[1]
    assert l % tm == 0
    tok = lambda w: pl.BlockSpec((1, tm, w), lambda i, j: (i, j, 0))
    per_b = pl.BlockSpec((1, m, d), lambda i, j: (i, 0, 0))
    full = lambda a: pl.BlockSpec(a.shape, lambda i, j: (0, 0))
    nq = norm_q.reshape(1, d)
    weights = (w_ssm, w_sb, w_out, nq, w_q, w_o)
    block_bytes = (4 * _nbytes((tm, d), F32) + _nbytes((tm, SSM_INNER + SB_INNER), BF16)
                   + 2 * _nbytes((m, d), F32) + sum(_nbytes(w.shape, w.dtype) for w in weights))
    return pl.pallas_call(
        functools.partial(_merge_mem_kernel, scale=MEM_HEAD_DIM ** -0.5),
        grid=(b, l // tm),
        in_specs=[tok(d), tok(SSM_INNER), tok(SB_INNER), tok(d), tok(d), per_b, per_b] + [full(w) for w in weights],
        out_specs=tok(d),
        out_shape=jax.ShapeDtypeStruct((b, l, d), F32),
        compiler_params=pltpu.CompilerParams(
            dimension_semantics=("parallel", "parallel"), vmem_limit_bytes=_vmem_limit(block_bytes)),
        name="merge_mem",
    )(x, yn, ysb, gs, gb, mk, mv, *weights)


def _route(logits):
    lane = lax.broadcasted_iota(jnp.int32, logits.shape, 1)
    big = jnp.int32(LANES_V7X)
    neg = -jnp.inf
    is_group = lane < MOE_GROUPS
    gl = jnp.where(is_group, logits, neg)
    ge = jnp.where(is_group, jnp.exp(gl - jnp.max(gl, axis=-1, keepdims=True)), 0.0)
    pg = ge / jnp.sum(ge, axis=-1, keepdims=True)
    p_sel = jnp.max(pg, axis=-1, keepdims=True)
    g_sel = jnp.min(jnp.where(is_group & (pg == p_sel), lane, big), axis=-1, keepdims=True)
    e_idx = lane - MOE_GROUPS
    in_group = (e_idx >= 0) & (e_idx < N_EXPERTS) & ((e_idx // EXPERTS_PER_GROUP) == g_sel)
    le = jnp.where(in_group, logits, neg)
    v1 = jnp.max(le, axis=-1, keepdims=True)
    i1 = jnp.min(jnp.where(in_group & (le == v1), lane, big), axis=-1, keepdims=True)
    rest = in_group & (lane != i1)
    le2 = jnp.where(rest, logits, neg)
    v2 = jnp.max(le2, axis=-1, keepdims=True)
    i2 = jnp.min(jnp.where(rest & (le2 == v2), lane, big), axis=-1, keepdims=True)
    e2 = jnp.exp(v2 - v1)
    den = 1.0 + e2
    dense_w = (jnp.where(lane == i1, (1.0 / den) * p_sel, 0.0)
               + jnp.where(lane == i2, (e2 / den) * p_sel, 0.0))
    return dense_w, g_sel


def _moe_hidden(x_ref, nf_ref):
    x = x_ref[...]
    return ((x * _rms_scale(x)) * nf_ref[...]).astype(BF16)


def _moe_rank_kernel(x_ref, nf_ref, wr_ref, info_ref, cnt_ref, carry_ref):
    i = pl.program_id(0)

    @pl.when(i == 0)
    def _():
        carry_ref[...] = jnp.zeros_like(carry_ref)

    hb = _moe_hidden(x_ref, nf_ref)
    _, g_sel = _route(jnp.dot(hb, wr_ref[...], preferred_element_type=F32))
    tm = hb.shape[0]
    lane = lax.broadcasted_iota(jnp.int32, (tm, LANES_V7X), 1)
    onehot = lane == g_sel
    r = lax.broadcasted_iota(jnp.int32, (tm, tm), 0)
    c = lax.broadcasted_iota(jnp.int32, (tm, tm), 1)
    before = jnp.dot((c < r).astype(BF16), onehot.astype(BF16), preferred_element_type=F32) + carry_ref[...]
    rank = jnp.sum(jnp.where(onehot, before, 0.0), axis=-1, keepdims=True)
    info = jnp.where(lane == 0, g_sel.astype(F32), jnp.where(lane == 1, rank, 0.0))
    info_ref[...] = info.astype(jnp.int32)
    carry_ref[...] += jnp.sum(onehot.astype(F32), axis=0, keepdims=True)
    cnt_ref[...] = carry_ref[...].astype(jnp.int32)


def _moe_experts_kernel(tile_ref, exp_ref, first_ref, last_ref, act_ref, x_ref, nf_ref, wr_ref, wg_ref, wu_ref,
                        wd_ref, nfin_ref, y_ref, h_ref, dw_ref, acc_ref, *, final_norm):
    w = pl.program_id(0)

    @pl.when(first_ref[w] == 1)
    def _():
        hb = _moe_hidden(x_ref, nf_ref)
        h_ref[...] = hb
        dw_ref[...] = _route(jnp.dot(hb, wr_ref[...], preferred_element_type=F32))[0]
        acc_ref[...] = jnp.zeros_like(acc_ref)

    @pl.when(act_ref[w] == 1)
    def _():
        hb = h_ref[...]
        gate = jnp.dot(hb, wg_ref[0], preferred_element_type=F32)
        up = jnp.dot(hb, wu_ref[0], preferred_element_type=F32)
        he = (gate * _sigmoid(gate)) * up
        down = jnp.dot(he.astype(BF16), wd_ref[0], preferred_element_type=F32)
        lane = lax.broadcasted_iota(jnp.int32, dw_ref.shape, 1)
        dcol = jnp.sum(jnp.where(lane == exp_ref[w] + MOE_GROUPS, dw_ref[...], 0.0), axis=-1, keepdims=True)
        acc_ref[...] += dcol * down

    @pl.when(last_ref[w] == 1)
    def _():
        xo = x_ref[...] + acc_ref[...]
        y_ref[...] = (xo * _rms_scale(xo)) * nfin_ref[...] if final_norm else xo


def _permute_rows_kernel(idx_ref, src_ref, dst_ref, sem, *, rows, scatter):
    def row_copy(r, k):
        src_row, dst_row = (r, k) if scatter else (k, r)
        return pltpu.make_async_copy(src_ref.at[pl.ds(src_row, 1)], dst_ref.at[pl.ds(dst_row, 1)], sem.at[0])

    def start(r, carry):
        row_copy(r, idx_ref[0, 0, r]).start()
        return carry

    def wait(r, carry):
        row_copy(0, 0).wait()
        return carry

    lax.fori_loop(0, rows, start, 0, unroll=8)
    lax.fori_loop(0, rows, wait, 0, unroll=8)


def _permute_rows(src, idx, scatter, name):
    t, d = src.shape
    rows = _tile(t, PERMUTE_ROWS)
    tile = pl.BlockSpec((rows, d), lambda s: (s, 0))
    hbm = pl.BlockSpec(memory_space=pl.ANY)
    return pl.pallas_call(
        functools.partial(_permute_rows_kernel, rows=rows, scatter=scatter),
        grid=(t // rows,),
        in_specs=[pl.BlockSpec((1, 1, rows), lambda s: (s, 0, 0), memory_space=pltpu.SMEM),
                  tile if scatter else hbm],
        out_specs=hbm if scatter else tile,
        out_shape=jax.ShapeDtypeStruct((t, d), src.dtype),
        scratch_shapes=[pltpu.SemaphoreType.DMA((1,))],
        compiler_params=pltpu.CompilerParams(
            dimension_semantics=("arbitrary",), vmem_limit_bytes=_vmem_limit(_nbytes((rows, d), src.dtype))),
        name=name,
    )(idx.reshape(t // rows, 1, rows), src)


def _moe_plan(gid, rank, counts, n_tiles, tm):
    i32 = jnp.int32
    ends = jnp.cumsum(counts).astype(i32)
    pos = (ends - counts)[gid] + rank
    first_row = jnp.arange(n_tiles, dtype=i32) * tm
    groups_before = lambda row: jnp.sum((ends[None, :] <= row[:, None]).astype(i32), axis=1)
    g_lo, g_hi = groups_before(first_row), groups_before(first_row + (tm - 1))
    n_items = EXPERTS_PER_GROUP * (g_hi - g_lo + 1)
    item_end = jnp.cumsum(n_items).astype(i32)
    n_work = EXPERTS_PER_GROUP * (n_tiles + MOE_GROUPS - 1)
    w = jnp.arange(n_work, dtype=i32)
    tile = jnp.minimum(jnp.sum((item_end[None, :] <= w[:, None]).astype(i32), axis=1), n_tiles - 1)
    local = w - (item_end - n_items)[tile]
    active = w < item_end[-1]
    expert = jnp.where(active, EXPERTS_PER_GROUP * g_lo[tile] + local,
                       EXPERTS_PER_GROUP * g_hi[-1] + EXPERTS_PER_GROUP - 1)
    first = active & (local == 0)
    last = active & (local == n_items[tile] - 1)
    return pos.astype(i32), (tile, expert.astype(i32), first.astype(i32), last.astype(i32), active.astype(i32))


def _moe(x2d, norm_ffn, w_router, w_gate, w_up, w_down, norm_final, final_norm, tm):
    t, d = x2d.shape
    assert t % tm == 0 and t < 2 ** 24
    n_tiles = t // tm
    n_e, _, ff = w_gate.shape
    nf, nfin = norm_ffn.reshape(1, d), norm_final.reshape(1, d)
    const = lambda a: pl.BlockSpec(a.shape, lambda i, *_: (0, 0))

    rank_bytes = _nbytes((tm, d), F32) + _nbytes(w_router.shape, BF16) + _nbytes((tm, LANES_V7X), F32)
    info, cnt = pl.pallas_call(
        _moe_rank_kernel,
        grid=(n_tiles,),
        in_specs=[pl.BlockSpec((tm, d), lambda i: (i, 0)), const(nf), const(w_router)],
        out_specs=[pl.BlockSpec((tm, LANES_V7X), lambda i: (i, 0)), pl.BlockSpec((1, LANES_V7X), lambda i: (0, 0))],
        out_shape=[jax.ShapeDtypeStruct((t, LANES_V7X), jnp.int32), jax.ShapeDtypeStruct((1, LANES_V7X), jnp.int32)],
        scratch_shapes=[pltpu.VMEM((1, LANES_V7X), F32)],
        compiler_params=pltpu.CompilerParams(
            dimension_semantics=("arbitrary",), vmem_limit_bytes=_vmem_limit(rank_bytes)),
        name="moe_rank",
    )(x2d, nf, w_router)
    pos, plan = _moe_plan(info[:, 0], info[:, 1], cnt[0, :MOE_GROUPS], n_tiles, tm)

    xs = _permute_rows(x2d, pos, True, "moe_sort")
    by_tile = lambda wi, tile, *_: (tile[wi], 0)
    by_expert = lambda wi, tile, exp, *_: (exp[wi], 0, 0)
    block_bytes = (2 * _nbytes((tm, d), F32) + 2 * _nbytes((1, d), F32) + _nbytes(w_router.shape, BF16)
                   + 3 * _nbytes((d, ff), BF16))
    scratch_bytes = _nbytes((tm, d), BF16) + _nbytes((tm, LANES_V7X), F32) + _nbytes((tm, d), F32)
    ys = pl.pallas_call(
        functools.partial(_moe_experts_kernel, final_norm=final_norm),
        grid_spec=pltpu.PrefetchScalarGridSpec(
            num_scalar_prefetch=len(plan),
            grid=(plan[0].shape[0],),
            in_specs=[pl.BlockSpec((tm, d), by_tile), const(nf), const(w_router),
                      pl.BlockSpec((1, d, ff), by_expert), pl.BlockSpec((1, d, ff), by_expert),
                      pl.BlockSpec((1, ff, d), by_expert), const(nfin)],
            out_specs=pl.BlockSpec((tm, d), by_tile),
            scratch_shapes=[pltpu.VMEM((tm, d), BF16), pltpu.VMEM((tm, LANES_V7X), F32), pltpu.VMEM((tm, d), F32)]),
        out_shape=jax.ShapeDtypeStruct((t, d), F32),
        compiler_params=pltpu.CompilerParams(
            dimension_semantics=("arbitrary",), vmem_limit_bytes=_vmem_limit(block_bytes, scratch_bytes)),
        name="moe_experts",
    )(*plan, xs, nf, w_router, w_gate, w_up, w_down, nfin)
    return _permute_rows(ys, pos, False, "moe_unsort")


def _tile(n, pref):
    t = min(n, pref)
    while n % t:
        t //= 2
    return t


def _split_w_in(w_in):
    offs = [0]
    for s in IN_SIZES:
        offs.append(offs[-1] + s)
    return [w_in[:, offs[i]:offs[i + 1]].astype(BF16) for i in range(len(IN_SIZES))]


def _group_step(x, conv_ctx, ssm_h0, past_k, past_v, mem_k, mem_v, lw):
    b, l, d = x.shape
    t = b * l
    x2d = x.reshape(t, d)
    w_z, w_xbc, w_dt, w_q, w_k, w_v, w_gs, w_gb = lw["w_in_parts"]
    flat32 = ((F32, None),)
    kv_out = ((F32, SB_HEADS), (BF16, None))
    z, xbc, dt, q, k, kb, v, vb, gs, gb = _norm_proj(
        x2d, lw["norm_mix"], [w_z, w_xbc, w_dt, w_q, w_k, w_v, w_gs, w_gb],
        [flat32, flat32, flat32, ((BF16, None),), kv_out, kv_out, flat32, flat32],
        _tile(t, 256), "in_proj")
    r3 = lambda a: a.reshape(b, l, a.shape[-1])

    h0t = ssm_h0.transpose(0, 3, 1, 2).reshape(b, SSM_STATE, SSM_INNER)
    ctx8 = jnp.pad(conv_ctx, ((0, 0), (SUBLANES_V7X - (CONV_W - 1), 0), (0, 0)))
    yn, ht = _ssd(r3(xbc), r3(dt), r3(z), ctx8, h0t, lw["conv_w"], lw["conv_b"], lw["dt_bias"], lw["a_log"],
                  lw["d_skip"], lw["ssm_norm"], _tile(l, 128))
    ssm_new = ht.reshape(b, SSM_STATE, SSM_HEADS, SSM_HEAD_DIM).transpose(0, 2, 3, 1)
    assert l >= CONV_W - 1
    conv_new = r3(xbc)[:, l - (CONV_W - 1):, :]

    tq = _tile(l, SB_CHUNK)
    hps = min(SB_HEADS, SB_STACK_ROWS // tq)
    ysb = _sb_attn(r3(q), r3(kb), r3(vb), past_k, past_v, tq, SB_CHUNK, hps)

    x_mem = _merge_mem(x, yn, ysb, r3(gs), r3(gb), mem_k, mem_v, lw["w_ssm_br"], lw["w_sb_br"], lw["w_out"],
                       lw["norm_mem_q"], lw["w_mem_q"], lw["w_mem_o"], _tile(l, 512))
    kh = k.reshape(b, l, SB_HEADS, SB_HEAD_DIM)
    vh = v.reshape(b, l, SB_HEADS, SB_HEAD_DIM)
    return x_mem, conv_new, ssm_new, kh, vh


def kernel(x_prompt, x_sample, cache_sb_k, cache_sb_v, state_ssm, state_conv, cache_mem_k, cache_mem_v, mem_prompt, norm_mix, w_in, conv_w, conv_b, dt_bias, a_log, d_skip, ssm_norm, w_ssm_br, w_sb_br, w_out, norm_mem_q, norm_mem_kv, w_mem_q, w_mem_kv, w_mem_o, norm_ffn, w_group_router, w_expert_router, w_gate_e, w_up_e, w_down_e, norm_final):
    depth = norm_mix.shape[0]
    bp, lp, d = x_prompt.shape
    bs, ls, _ = x_sample.shape
    xp, xs = x_prompt, x_sample
    outs = {name: [] for name in ("skp", "svp", "ssp", "scp", "mkp", "mvp", "sks", "svs", "sss", "scs")}
    for l in range(depth):
        router = jnp.concatenate([w_group_router[l], w_expert_router[l]], axis=1)
        router = jnp.pad(router, ((0, 0), (0, LANES_V7X - router.shape[1]))).astype(BF16)
        lw = dict(
            norm_mix=norm_mix[l], w_in_parts=_split_w_in(w_in[l]), conv_w=conv_w[l], conv_b=conv_b[l],
            dt_bias=dt_bias[l], a_log=a_log[l], d_skip=d_skip[l], ssm_norm=ssm_norm[l],
            w_ssm_br=w_ssm_br[l].astype(BF16), w_sb_br=w_sb_br[l].astype(BF16), w_out=w_out[l].astype(BF16),
            norm_mem_q=norm_mem_q[l], w_mem_q=w_mem_q[l].astype(BF16), w_mem_o=w_mem_o[l].astype(BF16))
        moe_w = (norm_ffn[l], router, w_gate_e[l].astype(BF16), w_up_e[l].astype(BF16), w_down_e[l].astype(BF16))
        last = l == depth - 1

        m = mem_prompt.shape[1]
        w_kv = w_mem_kv[l].astype(BF16)
        mk_p, mv_p = _norm_proj(mem_prompt.reshape(bp * m, d), norm_mem_kv[l], [w_kv[:, :d], w_kv[:, d:]],
                                [((F32, None),), ((F32, None),)], _tile(bp * m, 256), "mem_kv")
        mk_p, mv_p = mk_p.reshape(bp, m, d), mv_p.reshape(bp, m, d)
        conv0 = jnp.zeros((bp, CONV_W - 1, CONV_DIM), xp.dtype)
        h0 = jnp.zeros((bp, SSM_HEADS, SSM_HEAD_DIM, SSM_STATE), xp.dtype)
        xp, conv_p, ssm_p, k_p, v_p = _group_step(xp, conv0, h0, None, None, mk_p, mv_p, lw)
        xp = _moe(xp.reshape(bp * lp, d), *moe_w, norm_final, last, _tile(bp * lp, MOE_TILE)).reshape(bp, lp, d)
        outs["skp"].append(k_p); outs["svp"].append(v_p); outs["ssp"].append(ssm_p); outs["scp"].append(conv_p)
        outs["mkp"].append(mk_p.reshape(bp, m, MEM_HEADS, MEM_HEAD_DIM))
        outs["mvp"].append(mv_p.reshape(bp, m, MEM_HEADS, MEM_HEAD_DIM))

        past = cache_sb_k.shape[2]
        ms = cache_mem_k.shape[2]
        xs, conv_s, ssm_s, k_s, v_s = _group_step(
            xs, state_conv[l], state_ssm[l], cache_sb_k[l].reshape(bs, past, SB_INNER),
            cache_sb_v[l].reshape(bs, past, SB_INNER), cache_mem_k[l].reshape(bs, ms, d),
            cache_mem_v[l].reshape(bs, ms, d), lw)
        xs = _moe(xs.reshape(bs * ls, d), *moe_w, norm_final, last, _tile(bs * ls, MOE_TILE)).reshape(bs, ls, d)
        outs["sks"].append(k_s); outs["svs"].append(v_s); outs["sss"].append(ssm_s); outs["scs"].append(conv_s)

    st = lambda name: jnp.stack(outs[name])
    return (xp, xs, st("skp"), st("svp"), st("ssp"), st("scp"), st("mkp"), st("mvp"),
            st("sks"), st("svs"), st("sss"), st("scs"))
```

```python
import functools

import jax
import jax.numpy as jnp
from jax import lax
from jax.experimental import pallas as pl
from jax.experimental.pallas import tpu as pltpu

F32 = jnp.float32
BF16 = jnp.bfloat16
EPS = 1e-6

D_MODEL = 1024
SSM_HEADS = 16
SSM_HEAD_DIM = 64
SSM_INNER = SSM_HEADS * SSM_HEAD_DIM
SSM_GROUPS = 2
SSM_STATE = 128
CONV_W = 4
CONV_DIM = SSM_INNER + 2 * SSM_GROUPS * SSM_STATE
SB_HEADS = 8
SB_HEAD_DIM = 64
SB_INNER = SB_HEADS * SB_HEAD_DIM
MEM_HEADS = 4
MEM_HEAD_DIM = D_MODEL // MEM_HEADS
MOE_GROUPS = 4
EXPERTS_PER_GROUP = 4
N_EXPERTS = MOE_GROUPS * EXPERTS_PER_GROUP
IN_SIZES = (SSM_INNER, CONV_DIM, SSM_HEADS, SB_INNER, SB_INNER, SB_INNER, D_MODEL, D_MODEL)

LANES_V7X = 128
SUBLANES_V7X = 8
VMEM_BYTES_V7X = 64 * 1024 * 1024
COMPILER_TEMP_BYTES = 20 * 1024 * 1024
VMEM_CAP_BYTES = 60000 * 1024

NT_DIMS = (((1,), (1,)), ((), ()))

SB_CHUNK = 512
SB_SUB = 256
SB_STACK_ROWS = 2048

MOE_TILE = 512
PERMUTE_ROWS = 2048


def _vmem_limit(block_bytes, scratch_bytes=0):
    need = 2 * block_bytes + scratch_bytes + COMPILER_TEMP_BYTES
    return int(min(need, VMEM_CAP_BYTES))


def _nbytes(shape, dtype):
    n = 1
    for s in shape:
        n *= s
    return n * jnp.dtype(dtype).itemsize


def _sigmoid(x):
    return 1.0 / (1.0 + jnp.exp(-x))


def _softplus(x):
    return jnp.maximum(x, 0.0) + jnp.log1p(jnp.exp(-jnp.abs(x)))


def _rms_scale(x):
    return lax.rsqrt(jnp.mean(x * x, axis=-1, keepdims=True) + EPS)


def _norm_proj_kernel(x_ref, g_ref, *refs, out_plan):
    n_w = len(out_plan)
    w_refs = refs[:n_w]
    o_refs = refs[n_w:]
    x = x_ref[...]
    h = ((x * _rms_scale(x)) * g_ref[...]).astype(BF16)
    oi = 0
    for w_ref, outs in zip(w_refs, out_plan):
        y = jnp.dot(h, w_ref[...], preferred_element_type=F32)
        for dt, _ in outs:
            o_ref = o_refs[oi]
            o_ref[...] = y.astype(dt).reshape(o_ref.shape)
            oi += 1


def _norm_proj(x2d, gain, weights, out_plan, tm, name):
    t, d = x2d.shape
    assert t % tm == 0
    in_specs = [pl.BlockSpec((tm, d), lambda i: (i, 0)), pl.BlockSpec((1, d), lambda i: (0, 0))]
    block_bytes = _nbytes((tm, d), F32) + _nbytes((1, d), F32)
    for w in weights:
        in_specs.append(pl.BlockSpec(w.shape, lambda i: (0, 0)))
        block_bytes += _nbytes(w.shape, w.dtype)
    out_shapes, out_specs = [], []
    for w, outs in zip(weights, out_plan):
        n = w.shape[1]
        for dt, heads in outs:
            if heads is None:
                out_shapes.append(jax.ShapeDtypeStruct((t, n), dt))
                out_specs.append(pl.BlockSpec((tm, n), lambda i: (i, 0)))
                block_bytes += _nbytes((tm, max(n, LANES_V7X)), dt)
            else:
                out_shapes.append(jax.ShapeDtypeStruct((t, heads, n // heads), dt))
                out_specs.append(pl.BlockSpec((tm, heads, n // heads), lambda i: (i, 0, 0)))
                block_bytes += _nbytes((tm, heads, max(n // heads, LANES_V7X)), dt)
    return pl.pallas_call(
        functools.partial(_norm_proj_kernel, out_plan=tuple(out_plan)),
        grid=(t // tm,),
        in_specs=in_specs,
        out_specs=out_specs,
        out_shape=out_shapes,
        compiler_params=pltpu.CompilerParams(
            dimension_semantics=("parallel",), vmem_limit_bytes=_vmem_limit(block_bytes)),
        name=name,
    )(x2d, gain.reshape(1, d), *weights)


def _ssd_kernel(xbc_ref, dt_ref, z_ref, ctx_ref, h0_ref, convw_ref, convb_ref, dtb_ref, alog_ref,
                dskip_ref, norm_ref, yn_ref, ht_ref, ext_ref, y_ref, *, c):
    j = pl.program_id(1)
    halo = SUBLANES_V7X
    n = SSM_STATE
    pair_w = 2 * SSM_HEAD_DIM

    @pl.when(j == 0)
    def _():
        ext_ref[0:halo, :] = ctx_ref[0]
        ht_ref[0] = h0_ref[0]

    raw = xbc_ref[0]
    ext_ref[halo:halo + c, :] = raw
    ext = ext_ref[...]
    conv = convb_ref[...]
    for tap in range(CONV_W - 1):
        shifted = pltpu.roll(ext, CONV_W - 1 - tap, axis=0)[halo:halo + c, :]
        conv = conv + shifted * convw_ref[tap:tap + 1, :]
    conv = conv + raw * convw_ref[CONV_W - 1:CONV_W, :]
    ext_ref[0:halo, :] = ext_ref[c:c + halo, :]
    act = conv * _sigmoid(conv)

    dt = _softplus(dt_ref[0] + dtb_ref[...])
    a = -jnp.exp(alog_ref[...])
    row = lax.broadcasted_iota(jnp.int32, (c, c), 0)
    col = lax.broadcasted_iota(jnp.int32, (c, c), 1)
    tril = row >= col
    cum = jnp.dot(tril.astype(F32), dt * a, precision=lax.Precision.HIGHEST,
                  preferred_element_type=F32)
    hrow = lax.broadcasted_iota(jnp.int32, (SSM_HEADS, SSM_HEADS), 0)
    hcol = lax.broadcasted_iota(jnp.int32, (SSM_HEADS, SSM_HEADS), 1)
    eye_h = (hrow == hcol).astype(F32)
    cum_t = lax.dot_general(eye_h, cum, NT_DIMS, precision=lax.Precision.HIGHEST,
                            preferred_element_type=F32)
    cum_last = cum[c - 1:c, :]
    exp_cum = jnp.exp(cum)
    dt_end = dt * jnp.exp(cum_last - cum)
    chunk_decay = jnp.exp(cum_last)

    nrow = lax.broadcasted_iota(jnp.int32, (n, n), 0)
    ncol = lax.broadcasted_iota(jnp.int32, (n, n), 1)
    eye_n = (nrow == ncol).astype(BF16)
    lane_c = lax.broadcasted_iota(jnp.int32, (c, pair_w), 1) < SSM_HEAD_DIM
    lane_n = lax.broadcasted_iota(jnp.int32, (n, pair_w), 1) < SSM_HEAD_DIM
    lane_1 = lax.broadcasted_iota(jnp.int32, (1, pair_w), 1) < SSM_HEAD_DIM

    heads_per_group = SSM_HEADS // SSM_GROUPS
    for g in range(SSM_GROUPS):
        b_lo = SSM_INNER + g * n
        c_lo = SSM_INNER + SSM_GROUPS * n + g * n
        bm = act[:, b_lo:b_lo + n].astype(BF16)
        cm = act[:, c_lo:c_lo + n]
        cb = lax.dot_general(cm.astype(BF16), bm, NT_DIMS, preferred_element_type=F32)
        bm_t = lax.dot_general(eye_n, bm, NT_DIMS, preferred_element_type=F32).astype(BF16)
        for pair in range(heads_per_group // 2):
            ha = g * heads_per_group + 2 * pair
            hb = ha + 1
            lo = ha * SSM_HEAD_DIM
            x_pair = act[:, lo:lo + pair_w]
            st = ht_ref[0, :, lo:lo + pair_w]
            dt_pair = jnp.where(lane_c, dt[:, ha:ha + 1], dt[:, hb:hb + 1])
            dte_pair = jnp.where(lane_c, dt_end[:, ha:ha + 1], dt_end[:, hb:hb + 1])
            rhs = jnp.concatenate([(x_pair * dt_pair).astype(BF16), st.astype(BF16)], axis=0)
            ys = []
            for h in (ha, hb):
                seg = cum[:, h:h + 1] - cum_t[h:h + 1, :]
                decay = jnp.exp(jnp.where(tril, seg, -jnp.inf))
                lhs = jnp.concatenate(
                    [(cb * decay).astype(BF16), (cm * exp_cum[:, h:h + 1]).astype(BF16)], axis=1)
                ys.append(jnp.dot(lhs, rhs, preferred_element_type=F32))
            y_ref[:, lo:lo + pair_w] = jnp.where(lane_c, ys[0], ys[1])
            upd = jnp.dot(bm_t, (x_pair * dte_pair).astype(BF16), preferred_element_type=F32)
            cd_pair = jnp.where(lane_1, chunk_decay[:, ha:ha + 1], chunk_decay[:, hb:hb + 1])
            ht_ref[0, :, lo:lo + pair_w] = st * cd_pair + upd

    y = y_ref[...] + dskip_ref[...] * act[:, :SSM_INNER]
    zz = z_ref[0]
    gated = y * (zz * _sigmoid(zz))
    yn_ref[0] = ((gated * _rms_scale(gated)) * norm_ref[...]).astype(yn_ref.dtype)


def _ssd(xbc, dt, z, ctx8, h0t, conv_w, conv_b, dt_bias, a_log, d_skip, ssm_norm, c):
    b, l, _ = xbc.shape
    assert l % c == 0 and c % SUBLANES_V7X == 0
    n = SSM_STATE
    per_b = lambda shape: pl.BlockSpec(shape, lambda i, j: (i, 0, 0))
    chunk = lambda w: pl.BlockSpec((1, c, w), lambda i, j: (i, j, 0))
    full = lambda shape: pl.BlockSpec(shape, lambda i, j: (0, 0))
    block_bytes = (_nbytes((c, CONV_DIM), F32) + _nbytes((c, LANES_V7X), F32) + _nbytes((c, SSM_INNER), F32)
                   + _nbytes((8, CONV_DIM), F32) + 2 * _nbytes((n, SSM_INNER), F32)
                   + _nbytes((c, SSM_INNER), BF16) + 8 * _nbytes((8, CONV_DIM), F32))
    scratch_bytes = _nbytes((c + 8, CONV_DIM), F32) + _nbytes((c, SSM_INNER), F32)
    return pl.pallas_call(
        functools.partial(_ssd_kernel, c=c),
        grid=(b, l // c),
        in_specs=[chunk(CONV_DIM), chunk(SSM_HEADS), chunk(SSM_INNER), per_b((1, 8, CONV_DIM)),
                  per_b((1, n, SSM_INNER)), full((CONV_W, CONV_DIM)), full((1, CONV_DIM)),
                  full((1, SSM_HEADS)), full((1, SSM_HEADS)), full((1, SSM_INNER)), full((1, SSM_INNER))],
        out_specs=[chunk(SSM_INNER), per_b((1, n, SSM_INNER))],
        out_shape=[jax.ShapeDtypeStruct((b, l, SSM_INNER), BF16),
                   jax.ShapeDtypeStruct((b, n, SSM_INNER), F32)],
        scratch_shapes=[pltpu.VMEM((c + 8, CONV_DIM), F32), pltpu.VMEM((c, SSM_INNER), F32)],
        compiler_params=pltpu.CompilerParams(
            dimension_semantics=("parallel", "arbitrary"),
            vmem_limit_bytes=_vmem_limit(block_bytes, scratch_bytes)),
        name="ssd",
    )(xbc, dt, z, ctx8, h0t, conv_w, conv_b.reshape(1, -1), dt_bias.reshape(1, -1), a_log.reshape(1, -1),
      jnp.repeat(d_skip, SSM_HEAD_DIM).reshape(1, -1), ssm_norm.reshape(1, -1))


def _sb_attn_kernel(*refs, tq, tkp, n_past, hps, scale):
    if n_past:
        q_ref, kn_ref, vn_ref, kp_ref, vp_ref, o_ref, acc_ref, run_ref = refs
    else:
        q_ref, kn_ref, vn_ref, o_ref, acc_ref, run_ref, w_ref = refs
    qi = pl.program_id(2)
    width = hps * SB_HEAD_DIM
    m = hps * tq
    q = (q_ref[0].astype(F32) * scale).astype(BF16)
    head_of_lane = lax.broadcasted_iota(jnp.int32, (tq, width), 1) // SB_HEAD_DIM
    zero = jnp.zeros_like(q)
    q_stack = jnp.concatenate([jnp.where(head_of_lane == h, q, zero) for h in range(hps)], axis=0)

    def suffix_ones(t):
        r = lax.broadcasted_iota(jnp.int32, (t, t), 0)
        s = lax.broadcasted_iota(jnp.int32, (t, t), 1)
        return jnp.where(r > s, -1.0, 0.0).astype(BF16)

    def init():
        acc_ref[...] = jnp.zeros_like(acc_ref)
        run_ref[...] = jnp.zeros_like(run_ref)

    sign_bit = jnp.int32(-2 ** 31)

    def apply(w, v):
        acc_ref[...] += jnp.dot(w, v, preferred_element_type=F32)

    def weights(k, mask, due=None):
        ck = k.shape[0]
        sub = min(ck, SB_SUB)
        ones = suffix_ones(sub)
        zs = lax.dot_general(q_stack, k, NT_DIMS, preferred_element_type=F32)
        neg_abs = lax.bitcast_convert_type(lax.bitcast_convert_type(zs, jnp.int32) | sign_bit, F32)
        drop = jnp.maximum(zs, 0.0) + jnp.log(1.0 + jnp.exp(neg_abs))
        if mask is not None:
            drop = jnp.where(mask, drop, 0.0)
        drop_b = drop.astype(BF16)
        log_beta = zs - drop
        if due is not None:
            apply(*due)
        run = run_ref[...]
        ws = []
        for j in reversed(range(ck // sub)):
            sl = slice(j * sub, (j + 1) * sub)
            suffix = jnp.dot(drop_b[:, sl], ones, preferred_element_type=F32)
            w = jnp.exp(log_beta[:, sl] + suffix + run)
            if mask is not None:
                w = jnp.where(mask[:, sl], w, 0.0)
            ws.insert(0, w.astype(BF16))
            first = drop_b[:, j * sub:j * sub + 1].astype(F32)
            run = run + (suffix[:, 0:1] - first)
        run_ref[...] = run
        return ws[0] if len(ws) == 1 else jnp.concatenate(ws, axis=1)

    def new_chunk(ref, c):
        return ref[0, pl.ds(pl.multiple_of(c * tq, tq), tq), :]

    r = lax.rem(lax.broadcasted_iota(jnp.int32, (m, tq), 0), tq)
    s = lax.broadcasted_iota(jnp.int32, (m, tq), 1)

    def finish():
        out = acc_ref[0:tq, :]
        for h in range(1, hps):
            out = jnp.where(head_of_lane == h, acc_ref[h * tq:(h + 1) * tq, :], out)
        o_ref[0] = out.astype(o_ref.dtype)

    if n_past:
        c = pl.program_id(3)

        @pl.when(c == 0)
        def _():
            init()
            apply(weights(new_chunk(kn_ref, qi), s < r), new_chunk(vn_ref, qi))

            def new_body(i, carry):
                apply(weights(new_chunk(kn_ref, qi - 1 - i), None), new_chunk(vn_ref, qi - 1 - i))
                return carry

            lax.fori_loop(0, qi, new_body, 0)

        @pl.when(c > 0)
        def _():
            apply(weights(kp_ref[0].reshape(tkp, width).astype(BF16), None),
                  vp_ref[0].reshape(tkp, width).astype(BF16))

        @pl.when(c == n_past)
        def _():
            finish()
    else:
        init()
        w_ref[...] = weights(new_chunk(kn_ref, qi), s < r)

        def new_body(i, carry):
            due = (w_ref[...], new_chunk(vn_ref, qi - i))
            w_ref[...] = weights(new_chunk(kn_ref, qi - 1 - i), None, due)
            return carry

        lax.fori_loop(0, qi, new_body, 0)
        apply(w_ref[...], new_chunk(vn_ref, 0))
        finish()


def _sb_attn(q, k_new, v_new, k_past, v_past, tq, tkp, hps):
    b, l, _ = q.shape
    assert l % tq == 0 and SB_HEADS % hps == 0
    width = hps * SB_HEAD_DIM
    m = hps * tq
    n_past = 0
    grid = (b, SB_HEADS // hps, l // tq)
    tile = pl.BlockSpec((1, tq, width), lambda i, hq, qi, *_: (i, qi, hq))
    whole = pl.BlockSpec((1, l, width), lambda i, hq, qi, *_: (i, 0, hq))
    in_specs = [tile, whole, whole]
    args = [q, k_new, v_new]
    block_bytes = 2 * _nbytes((tq, width), BF16) + 2 * _nbytes((l, width), BF16)
    semantics = ("parallel", "parallel", "arbitrary")
    if k_past is not None:
        p = k_past.shape[1]
        assert p % tkp == 0 and hps == SB_HEADS
        n_past = p // tkp
        grid += (1 + n_past,)
        semantics += ("arbitrary",)
        past = pl.BlockSpec((1, tkp, SB_HEADS, SB_HEAD_DIM),
                            lambda i, hq, qi, c: (i, jnp.minimum(n_past - c, n_past - 1), 0, 0))
        in_specs += [past, past]
        args += [k_past, v_past]
        block_bytes += 2 * _nbytes((tkp, SB_HEADS, LANES_V7X), F32)
    scratch = [pltpu.VMEM((m, width), F32), pltpu.VMEM((m, 1), F32)]
    scratch_bytes = _nbytes((m, width), F32) + _nbytes((m, LANES_V7X), F32)
    if not n_past:
        scratch.append(pltpu.VMEM((m, tq), BF16))
        scratch_bytes += _nbytes((m, tq), BF16)
    return pl.pallas_call(
        functools.partial(_sb_attn_kernel, tq=tq, tkp=tkp, n_past=n_past, hps=hps, scale=SB_HEAD_DIM ** -0.5),
        grid=grid,
        in_specs=in_specs,
        out_specs=tile,
        out_shape=jax.ShapeDtypeStruct((b, l, SB_INNER), BF16),
        scratch_shapes=scratch,
        compiler_params=pltpu.CompilerParams(
            dimension_semantics=semantics,
            vmem_limit_bytes=_vmem_limit(block_bytes, scratch_bytes)),
        name="sb_attn",
    )(*args)


def _merge_mem_kernel(x_ref, yn_ref, ysb_ref, gs_ref, gb_ref, mk_ref, mv_ref, wssm_ref, wsb_ref, wout_ref,
                      nq_ref, wq_ref, wo_ref, o_ref, *, scale):
    y_ssm = jnp.dot(yn_ref[0], wssm_ref[...], preferred_element_type=F32)
    y_sb = jnp.dot(ysb_ref[0], wsb_ref[...], preferred_element_type=F32)
    merged = _sigmoid(gs_ref[0]) * y_ssm + _sigmoid(gb_ref[0]) * y_sb
    x1 = x_ref[0] + jnp.dot(merged.astype(BF16), wout_ref[...], preferred_element_type=F32)
    hq = ((x1 * _rms_scale(x1)) * nq_ref[...]).astype(BF16)
    q = jnp.dot(hq, wq_ref[...], preferred_element_type=F32)
    outs = []
    for h in range(MEM_HEADS):
        sl = slice(h * MEM_HEAD_DIM, (h + 1) * MEM_HEAD_DIM)
        mk = mk_ref[0, :, sl].astype(BF16)
        mv = mv_ref[0, :, sl].astype(BF16)
        s = lax.dot_general(q[:, sl].astype(BF16), mk, NT_DIMS, preferred_element_type=F32) * scale
        e = jnp.exp(s - jnp.max(s, axis=-1, keepdims=True))
        p = e / jnp.sum(e, axis=-1, keepdims=True)
        outs.append(jnp.dot(p.astype(BF16), mv, preferred_element_type=F32))
    o = jnp.concatenate(outs, axis=1)
    o_ref[0] = x1 + jnp.dot(o.astype(BF16), wo_ref[...], preferred_element_type=F32)


def _merge_mem(x, yn, ysb, gs, gb, mk, mv, w_ssm, w_sb, w_out, norm_q, w_q, w_o, tm):
    b, l, d = x.shape
    m = mk.shape[1]
    assert l % tm == 0
    tok = lambda w: pl.BlockSpec((1, tm, w), lambda i, j: (i, j, 0))
    per_b = pl.BlockSpec((1, m, d), lambda i, j: (i, 0, 0))
    full = lambda a: pl.BlockSpec(a.shape, lambda i, j: (0, 0))
    nq = norm_q.reshape(1, d)
    weights = (w_ssm, w_sb, w_out, nq, w_q, w_o)
    block_bytes = (4 * _nbytes((tm, d), F32) + _nbytes((tm, SSM_INNER + SB_INNER), BF16)
                   + 2 * _nbytes((m, d), F32) + sum(_nbytes(w.shape, w.dtype) for w in weights))
    return pl.pallas_call(
        functools.partial(_merge_mem_kernel, scale=MEM_HEAD_DIM ** -0.5),
        grid=(b, l // tm),
        in_specs=[tok(d), tok(SSM_INNER), tok(SB_INNER), tok(d), tok(d), per_b, per_b] + [full(w) for w in weights],
        out_specs=tok(d),
        out_shape=jax.ShapeDtypeStruct((b, l, d), F32),
        compiler_params=pltpu.CompilerParams(
            dimension_semantics=("parallel", "parallel"), vmem_limit_bytes=_vmem_limit(block_bytes)),
        name="merge_mem",
    )(x, yn, ysb, gs, gb, mk, mv, *weights)


def _route(logits):
    lane = lax.broadcasted_iota(jnp.int32, logits.shape, 1)
    big = jnp.int32(LANES_V7X)
    neg = -jnp.inf
    is_group = lane < MOE_GROUPS
    gl = jnp.where(is_group, logits, neg)
    ge = jnp.where(is_group, jnp.exp(gl - jnp.max(gl, axis=-1, keepdims=True)), 0.0)
    pg = ge / jnp.sum(ge, axis=-1, keepdims=True)
    p_sel = jnp.max(pg, axis=-1, keepdims=True)
    g_sel = jnp.min(jnp.where(is_group & (pg == p_sel), lane, big), axis=-1, keepdims=True)
    e_idx = lane - MOE_GROUPS
    in_group = (e_idx >= 0) & (e_idx < N_EXPERTS) & ((e_idx // EXPERTS_PER_GROUP) == g_sel)
    le = jnp.where(in_group, logits, neg)
    v1 = jnp.max(le, axis=-1, keepdims=True)
    i1 = jnp.min(jnp.where(in_group & (le == v1), lane, big), axis=-1, keepdims=True)
    rest = in_group & (lane != i1)
    le2 = jnp.where(rest, logits, neg)
    v2 = jnp.max(le2, axis=-1, keepdims=True)
    i2 = jnp.min(jnp.where(rest & (le2 == v2), lane, big), axis=-1, keepdims=True)
    e2 = jnp.exp(v2 - v1)
    den = 1.0 + e2
    dense_w = (jnp.where(lane == i1, (1.0 / den) * p_sel, 0.0)
               + jnp.where(lane == i2, (e2 / den) * p_sel, 0.0))
    return dense_w, g_sel


def _moe_hidden(x_ref, nf_ref):
    x = x_ref[...]
    return ((x * _rms_scale(x)) * nf_ref[...]).astype(BF16)


def _moe_rank_kernel(x_ref, nf_ref, wr_ref, info_ref, cnt_ref, carry_ref):
    i = pl.program_id(0)

    @pl.when(i == 0)
    def _():
        carry_ref[...] = jnp.zeros_like(carry_ref)

    hb = _moe_hidden(x_ref, nf_ref)
    _, g_sel = _route(jnp.dot(hb, wr_ref[...], preferred_element_type=F32))
    tm = hb.shape[0]
    lane = lax.broadcasted_iota(jnp.int32, (tm, LANES_V7X), 1)
    onehot = lane == g_sel
    r = lax.broadcasted_iota(jnp.int32, (tm, tm), 0)
    c = lax.broadcasted_iota(jnp.int32, (tm, tm), 1)
    before = jnp.dot((c < r).astype(BF16), onehot.astype(BF16), preferred_element_type=F32) + carry_ref[...]
    rank = jnp.sum(jnp.where(onehot, before, 0.0), axis=-1, keepdims=True)
    info = jnp.where(lane == 0, g_sel.astype(F32), jnp.where(lane == 1, rank, 0.0))
    info_ref[...] = info.astype(jnp.int32)
    carry_ref[...] += jnp.sum(onehot.astype(F32), axis=0, keepdims=True)
    cnt_ref[...] = carry_ref[...].astype(jnp.int32)


def _moe_experts_kernel(tile_ref, exp_ref, first_ref, last_ref, act_ref, x_ref, nf_ref, wr_ref, wg_ref, wu_ref,
                        wd_ref, nfin_ref, y_ref, h_ref, dw_ref, acc_ref, *, final_norm):
    w = pl.program_id(0)

    @pl.when(first_ref[w] == 1)
    def _():
        hb = _moe_hidden(x_ref, nf_ref)
        h_ref[...] = hb
        dw_ref[...] = _route(jnp.dot(hb, wr_ref[...], preferred_element_type=F32))[0]
        acc_ref[...] = jnp.zeros_like(acc_ref)

    @pl.when(act_ref[w] == 1)
    def _():
        hb = h_ref[...]
        gate = jnp.dot(hb, wg_ref[0], preferred_element_type=F32)
        up = jnp.dot(hb, wu_ref[0], preferred_element_type=F32)
        he = (gate * _sigmoid(gate)) * up
        down = jnp.dot(he.astype(BF16), wd_ref[0], preferred_element_type=F32)
        lane = lax.broadcasted_iota(jnp.int32, dw_ref.shape, 1)
        dcol = jnp.sum(jnp.where(lane == exp_ref[w] + MOE_GROUPS, dw_ref[...], 0.0), axis=-1, keepdims=True)
        acc_ref[...] += dcol * down

    @pl.when(last_ref[w] == 1)
    def _():
        xo = x_ref[...] + acc_ref[...]
        y_ref[...] = (xo * _rms_scale(xo)) * nfin_ref[...] if final_norm else xo


def _permute_rows_kernel(idx_ref, src_ref, dst_ref, sem, *, rows, scatter):
    def row_copy(r, k):
        src_row, dst_row = (r, k) if scatter else (k, r)
        return pltpu.make_async_copy(src_ref.at[pl.ds(src_row, 1)], dst_ref.at[pl.ds(dst_row, 1)], sem.at[0])

    def start(r, carry):
        row_copy(r, idx_ref[0, 0, r]).start()
        return carry

    def wait(r, carry):
        row_copy(0, 0).wait()
        return carry

    lax.fori_loop(0, rows, start, 0, unroll=8)
    lax.fori_loop(0, rows, wait, 0, unroll=8)


def _permute_rows(src, idx, scatter, name):
    t, d = src.shape
    rows = _tile(t, PERMUTE_ROWS)
    tile = pl.BlockSpec((rows, d), lambda s: (s, 0))
    hbm = pl.BlockSpec(memory_space=pl.ANY)
    return pl.pallas_call(
        functools.partial(_permute_rows_kernel, rows=rows, scatter=scatter),
        grid=(t // rows,),
        in_specs=[pl.BlockSpec((1, 1, rows), lambda s: (s, 0, 0), memory_space=pltpu.SMEM),
                  tile if scatter else hbm],
        out_specs=hbm if scatter else tile,
        out_shape=jax.ShapeDtypeStruct((t, d), src.dtype),
        scratch_shapes=[pltpu.SemaphoreType.DMA((1,))],
        compiler_params=pltpu.CompilerParams(
            dimension_semantics=("arbitrary",), vmem_limit_bytes=_vmem_limit(_nbytes((rows, d), src.dtype))),
        name=name,
    )(idx.reshape(t // rows, 1, rows), src)


def _moe_plan(gid, rank, counts, n_tiles, tm):
    i32 = jnp.int32
    ends = jnp.cumsum(counts).astype(i32)
    pos = (ends - counts)[gid] + rank
    first_row = jnp.arange(n_tiles, dtype=i32) * tm
    groups_before = lambda row: jnp.sum((ends[None, :] <= row[:, None]).astype(i32), axis=1)
    g_lo, g_hi = groups_before(first_row), groups_before(first_row + (tm - 1))
    n_items = EXPERTS_PER_GROUP * (g_hi - g_lo + 1)
    item_end = jnp.cumsum(n_items).astype(i32)
    n_work = EXPERTS_PER_GROUP * (n_tiles + MOE_GROUPS - 1)
    w = jnp.arange(n_work, dtype=i32)
    tile = jnp.minimum(jnp.sum((item_end[None, :] <= w[:, None]).astype(i32), axis=1), n_tiles - 1)
    local = w - (item_end - n_items)[tile]
    active = w < item_end[-1]
    expert = jnp.where(active, EXPERTS_PER_GROUP * g_lo[tile] + local,
                       EXPERTS_PER_GROUP * g_hi[-1] + EXPERTS_PER_GROUP - 1)
    first = active & (local == 0)
    last = active & (local == n_items[tile] - 1)
    return pos.astype(i32), (tile, expert.astype(i32), first.astype(i32), last.astype(i32), active.astype(i32))


def _moe(x2d, norm_ffn, w_router, w_gate, w_up, w_down, norm_final, final_norm, tm):
    t, d = x2d.shape
    assert t % tm == 0 and t < 2 ** 24
    n_tiles = t // tm
    n_e, _, ff = w_gate.shape
    nf, nfin = norm_ffn.reshape(1, d), norm_final.reshape(1, d)
    const = lambda a: pl.BlockSpec(a.shape, lambda i, *_: (0, 0))

    rank_bytes = _nbytes((tm, d), F32) + _nbytes(w_router.shape, BF16) + _nbytes((tm, LANES_V7X), F32)
    info, cnt = pl.pallas_call(
        _moe_rank_kernel,
        grid=(n_tiles,),
        in_specs=[pl.BlockSpec((tm, d), lambda i: (i, 0)), const(nf), const(w_router)],
        out_specs=[pl.BlockSpec((tm, LANES_V7X), lambda i: (i, 0)), pl.BlockSpec((1, LANES_V7X), lambda i: (0, 0))],
        out_shape=[jax.ShapeDtypeStruct((t, LANES_V7X), jnp.int32), jax.ShapeDtypeStruct((1, LANES_V7X), jnp.int32)],
        scratch_shapes=[pltpu.VMEM((1, LANES_V7X), F32)],
        compiler_params=pltpu.CompilerParams(
            dimension_semantics=("arbitrary",), vmem_limit_bytes=_vmem_limit(rank_bytes)),
        name="moe_rank",
    )(x2d, nf, w_router)
    pos, plan = _moe_plan(info[:, 0], info[:, 1], cnt[0, :MOE_GROUPS], n_tiles, tm)

    xs = _permute_rows(x2d, pos, True, "moe_sort")
    by_tile = lambda wi, tile, *_: (tile[wi], 0)
    by_expert = lambda wi, tile, exp, *_: (exp[wi], 0, 0)
    block_bytes = (2 * _nbytes((tm, d), F32) + 2 * _nbytes((1, d), F32) + _nbytes(w_router.shape, BF16)
                   + 3 * _nbytes((d, ff), BF16))
    scratch_bytes = _nbytes((tm, d), BF16) + _nbytes((tm, LANES_V7X), F32) + _nbytes((tm, d), F32)
    ys = pl.pallas_call(
        functools.partial(_moe_experts_kernel, final_norm=final_norm),
        grid_spec=pltpu.PrefetchScalarGridSpec(
            num_scalar_prefetch=len(plan),
            grid=(plan[0].shape[0],),
            in_specs=[pl.BlockSpec((tm, d), by_tile), const(nf), const(w_router),
                      pl.BlockSpec((1, d, ff), by_expert), pl.BlockSpec((1, d, ff), by_expert),
                      pl.BlockSpec((1, ff, d), by_expert), const(nfin)],
            out_specs=pl.BlockSpec((tm, d), by_tile),
            scratch_shapes=[pltpu.VMEM((tm, d), BF16), pltpu.VMEM((tm, LANES_V7X), F32), pltpu.VMEM((tm, d), F32)]),
        out_shape=jax.ShapeDtypeStruct((t, d), F32),
        compiler_params=pltpu.CompilerParams(
            dimension_semantics=("arbitrary",), vmem_limit_bytes=_vmem_limit(block_bytes, scratch_bytes)),
        name="moe_experts",
    )(*plan, xs, nf, w_router, w_gate, w_up, w_down, nfin)
    return _permute_rows(ys, pos, False, "moe_unsort")


def _tile(n, pref):
    t = min(n, pref)
    while n % t:
        t //= 2
    return t


def _split_w_in(w_in):
    offs = [0]
    for s in IN_SIZES:
        offs.append(offs[-1] + s)
    return [w_in[:, offs[i]:offs[i + 1]].astype(BF16) for i in range(len(IN_SIZES))]


def _group_step(x, conv_ctx, ssm_h0, past_k, past_v, mem_k, mem_v, lw):
    b, l, d = x.shape
    t = b * l
    x2d = x.reshape(t, d)
    w_z, w_xbc, w_dt, w_q, w_k, w_v, w_gs, w_gb = lw["w_in_parts"]
    flat32 = ((F32, None),)
    kv_out = ((F32, SB_HEADS), (BF16, None))
    z, xbc, dt, q, k, kb, v, vb, gs, gb = _norm_proj(
        x2d, lw["norm_mix"], [w_z, w_xbc, w_dt, w_q, w_k, w_v, w_gs, w_gb],
        [flat32, flat32, flat32, ((BF16, None),), kv_out, kv_out, flat32, flat32],
        _tile(t, 256), "in_proj")
    r3 = lambda a: a.reshape(b, l, a.shape[-1])

    h0t = ssm_h0.transpose(0, 3, 1, 2).reshape(b, SSM_STATE, SSM_INNER)
    ctx8 = jnp.pad(conv_ctx, ((0, 0), (SUBLANES_V7X - (CONV_W - 1), 0), (0, 0)))
    yn, ht = _ssd(r3(xbc), r3(dt), r3(z), ctx8, h0t, lw["conv_w"], lw["conv_b"], lw["dt_bias"], lw["a_log"],
                  lw["d_skip"], lw["ssm_norm"], _tile(l, 128))
    ssm_new = ht.reshape(b, SSM_STATE, SSM_HEADS, SSM_HEAD_DIM).transpose(0, 2, 3, 1)
    assert l >= CONV_W - 1
    conv_new = r3(xbc)[:, l - (CONV_W - 1):, :]

    tq = _tile(l, SB_CHUNK)
    hps = min(SB_HEADS, SB_STACK_ROWS // tq)
    ysb = _sb_attn(r3(q), r3(kb), r3(vb), past_k, past_v, tq, SB_CHUNK, hps)

    x_mem = _merge_mem(x, yn, ysb, r3(gs), r3(gb), mem_k, mem_v, lw["w_ssm_br"], lw["w_sb_br"], lw["w_out"],
                       lw["norm_mem_q"], lw["w_mem_q"], lw["w_mem_o"], _tile(l, 512))
    kh = k.reshape(b, l, SB_HEADS, SB_HEAD_DIM)
    vh = v.reshape(b, l, SB_HEADS, SB_HEAD_DIM)
    return x_mem, conv_new, ssm_new, kh, vh


def kernel(x_prompt, x_sample, cache_sb_k, cache_sb_v, state_ssm, state_conv, cache_mem_k, cache_mem_v, mem_prompt, norm_mix, w_in, conv_w, conv_b, dt_bias, a_log, d_skip, ssm_norm, w_ssm_br, w_sb_br, w_out, norm_mem_q, norm_mem_kv, w_mem_q, w_mem_kv, w_mem_o, norm_ffn, w_group_router, w_expert_router, w_gate_e, w_up_e, w_down_e, norm_final):
    depth = norm_mix.shape[0]
    bp, lp, d = x_prompt.shape
    bs, ls, _ = x_sample.shape
    xp, xs = x_prompt, x_sample
    outs = {name: [] for name in ("skp", "svp", "ssp", "scp", "mkp", "mvp", "sks", "svs", "sss", "scs")}
    for l in range(depth):
        router = jnp.concatenate([w_group_router[l], w_expert_router[l]], axis=1)
        router = jnp.pad(router, ((0, 0), (0, LANES_V7X - router.shape[1]))).astype(BF16)
        lw = dict(
            norm_mix=norm_mix[l], w_in_parts=_split_w_in(w_in[l]), conv_w=conv_w[l], conv_b=conv_b[l],
            dt_bias=dt_bias[l], a_log=a_log[l], d_skip=d_skip[l], ssm_norm=ssm_norm[l],
            w_ssm_br=w_ssm_br[l].astype(BF16), w_sb_br=w_sb_br[l].astype(BF16), w_out=w_out[l].astype(BF16),
            norm_mem_q=norm_mem_q[l], w_mem_q=w_mem_q[l].astype(BF16), w_mem_o=w_mem_o[l].astype(BF16))
        moe_w = (norm_ffn[l], router, w_gate_e[l].astype(BF16), w_up_e[l].astype(BF16), w_down_e[l].astype(BF16))
        last = l == depth - 1

        m = mem_prompt.shape[1]
        w_kv = w_mem_kv[l].astype(BF16)
        mk_p, mv_p = _norm_proj(mem_prompt.reshape(bp * m, d), norm_mem_kv[l], [w_kv[:, :d], w_kv[:, d:]],
                                [((F32, None),), ((F32, None),)], _tile(bp * m, 256), "mem_kv")
        mk_p, mv_p = mk_p.reshape(bp, m, d), mv_p.reshape(bp, m, d)
        conv0 = jnp.zeros((bp, CONV_W - 1, CONV_DIM), xp.dtype)
        h0 = jnp.zeros((bp, SSM_HEADS, SSM_HEAD_DIM, SSM_STATE), xp.dtype)
        xp, conv_p, ssm_p, k_p, v_p = _group_step(xp, conv0, h0, None, None, mk_p, mv_p, lw)
        xp = _moe(xp.reshape(bp * lp, d), *moe_w, norm_final, last, _tile(bp * lp, MOE_TILE)).reshape(bp, lp, d)
        outs["skp"].append(k_p); outs["svp"].append(v_p); outs["ssp"].append(ssm_p); outs["scp"].append(conv_p)
        outs["mkp"].append(mk_p.reshape(bp, m, MEM_HEADS, MEM_HEAD_DIM))
        outs["mvp"].append(mv_p.reshape(bp, m, MEM_HEADS, MEM_HEAD_DIM))

        ms = cache_mem_k.shape[2]
        xs, conv_s, ssm_s, k_s, v_s = _group_step(
            xs, state_conv[l], state_ssm[l], cache_sb_k[l], cache_sb_v[l], cache_mem_k[l].reshape(bs, ms, d),
            cache_mem_v[l].reshape(bs, ms, d), lw)
        xs = _moe(xs.reshape(bs * ls, d), *moe_w, norm_final, last, _tile(bs * ls, MOE_TILE)).reshape(bs, ls, d)
        outs["sks"].append(k_s); outs["svs"].append(v_s); outs["sss"].append(ssm_s); outs["scs"].append(conv_s)

    st = lambda name: jnp.stack(outs[name])
    return (xp, xs, st("skp"), st("svp"), st("ssp"), st("scp"), st("mkp"), st("mvp"),
            st("sks"), st("svs"), st("sss"), st("scs"))
```

```python
import functools

import jax
import jax.numpy as jnp
from jax import lax
from jax.experimental import pallas as pl
from jax.experimental.pallas import tpu as pltpu

F32 = jnp.float32
BF16 = jnp.bfloat16
EPS = 1e-6

D_MODEL = 1024
SSM_HEADS = 16
SSM_HEAD_DIM = 64
SSM_INNER = SSM_HEADS * SSM_HEAD_DIM
SSM_GROUPS = 2
SSM_STATE = 128
CONV_W = 4
CONV_DIM = SSM_INNER + 2 * SSM_GROUPS * SSM_STATE
SB_HEADS = 8
SB_HEAD_DIM = 64
SB_INNER = SB_HEADS * SB_HEAD_DIM
MEM_HEADS = 4
MEM_HEAD_DIM = D_MODEL // MEM_HEADS
MOE_GROUPS = 4
EXPERTS_PER_GROUP = 4
N_EXPERTS = MOE_GROUPS * EXPERTS_PER_GROUP
IN_SIZES = (SSM_INNER, CONV_DIM, SSM_HEADS, SB_INNER, SB_INNER, SB_INNER, D_MODEL, D_MODEL)

LANES_V7X = 128
SUBLANES_V7X = 8
VMEM_BYTES_V7X = 64 * 1024 * 1024
DMA_PRIORITIES_V7X = 2
COMPILER_TEMP_BYTES = 20 * 1024 * 1024
VMEM_CAP_BYTES = 60000 * 1024

NT_DIMS = (((1,), (1,)), ((), ()))

SB_CHUNK = 512
SB_SUB = 256
SB_STACK_ROWS = 2048

MOE_TILE = 512
PERMUTE_ROWS = 2048


def _vmem_limit(block_bytes, scratch_bytes=0):
    need = 2 * block_bytes + scratch_bytes + COMPILER_TEMP_BYTES
    return int(min(need, VMEM_CAP_BYTES))


def _nbytes(shape, dtype):
    n = 1
    for s in shape:
        n *= s
    return n * jnp.dtype(dtype).itemsize


def _sigmoid(x):
    return 1.0 / (1.0 + jnp.exp(-x))


def _softplus(x):
    return jnp.maximum(x, 0.0) + jnp.log1p(jnp.exp(-jnp.abs(x)))


def _rms_scale(x):
    return lax.rsqrt(jnp.mean(x * x, axis=-1, keepdims=True) + EPS)


def _norm_proj_kernel(x_ref, g_ref, *refs, out_plan):
    n_w = len(out_plan)
    w_refs = refs[:n_w]
    o_refs = refs[n_w:]
    x = x_ref[...]
    h = ((x * _rms_scale(x)) * g_ref[...]).astype(BF16)
    oi = 0
    for w_ref, outs in zip(w_refs, out_plan):
        y = jnp.dot(h, w_ref[...], preferred_element_type=F32)
        for dt, _ in outs:
            o_ref = o_refs[oi]
            o_ref[...] = y.astype(dt).reshape(o_ref.shape)
            oi += 1


def _norm_proj(x2d, gain, weights, out_plan, tm, name):
    t, d = x2d.shape
    assert t % tm == 0
    in_specs = [pl.BlockSpec((tm, d), lambda i: (i, 0)), pl.BlockSpec((1, d), lambda i: (0, 0))]
    block_bytes = _nbytes((tm, d), F32) + _nbytes((1, d), F32)
    for w in weights:
        in_specs.append(pl.BlockSpec(w.shape, lambda i: (0, 0)))
        block_bytes += _nbytes(w.shape, w.dtype)
    out_shapes, out_specs = [], []
    for w, outs in zip(weights, out_plan):
        n = w.shape[1]
        for dt, heads in outs:
            if heads is None:
                out_shapes.append(jax.ShapeDtypeStruct((t, n), dt))
                out_specs.append(pl.BlockSpec((tm, n), lambda i: (i, 0)))
                block_bytes += _nbytes((tm, max(n, LANES_V7X)), dt)
            else:
                out_shapes.append(jax.ShapeDtypeStruct((t, heads, n // heads), dt))
                out_specs.append(pl.BlockSpec((tm, heads, n // heads), lambda i: (i, 0, 0)))
                block_bytes += _nbytes((tm, heads, max(n // heads, LANES_V7X)), dt)
    return pl.pallas_call(
        functools.partial(_norm_proj_kernel, out_plan=tuple(out_plan)),
        grid=(t // tm,),
        in_specs=in_specs,
        out_specs=out_specs,
        out_shape=out_shapes,
        compiler_params=pltpu.CompilerParams(
            dimension_semantics=("parallel",), vmem_limit_bytes=_vmem_limit(block_bytes)),
        name=name,
    )(x2d, gain.reshape(1, d), *weights)


def _ssd_kernel(xbc_ref, dt_ref, z_ref, ctx_ref, h0_ref, convw_ref, convb_ref, dtb_ref, alog_ref,
                dskip_ref, norm_ref, yn_ref, ht_ref, ext_ref, y_ref, *, c):
    j = pl.program_id(1)
    halo = SUBLANES_V7X
    n = SSM_STATE
    pair_w = 2 * SSM_HEAD_DIM

    @pl.when(j == 0)
    def _():
        ext_ref[0:halo, :] = ctx_ref[0]
        ht_ref[0] = h0_ref[0]

    raw = xbc_ref[0]
    ext_ref[halo:halo + c, :] = raw
    ext = ext_ref[...]
    conv = convb_ref[...]
    for tap in range(CONV_W - 1):
        shifted = pltpu.roll(ext, CONV_W - 1 - tap, axis=0)[halo:halo + c, :]
        conv = conv + shifted * convw_ref[tap:tap + 1, :]
    conv = conv + raw * convw_ref[CONV_W - 1:CONV_W, :]
    ext_ref[0:halo, :] = ext_ref[c:c + halo, :]
    act = conv * _sigmoid(conv)

    dt = _softplus(dt_ref[0] + dtb_ref[...])
    a = -jnp.exp(alog_ref[...])
    row = lax.broadcasted_iota(jnp.int32, (c, c), 0)
    col = lax.broadcasted_iota(jnp.int32, (c, c), 1)
    tril = row >= col
    cum = jnp.dot(tril.astype(F32), dt * a, precision=lax.Precision.HIGHEST,
                  preferred_element_type=F32)
    hrow = lax.broadcasted_iota(jnp.int32, (SSM_HEADS, SSM_HEADS), 0)
    hcol = lax.broadcasted_iota(jnp.int32, (SSM_HEADS, SSM_HEADS), 1)
    eye_h = (hrow == hcol).astype(F32)
    cum_t = lax.dot_general(eye_h, cum, NT_DIMS, precision=lax.Precision.HIGHEST,
                            preferred_element_type=F32)
    cum_last = cum[c - 1:c, :]
    exp_cum = jnp.exp(cum)
    dt_end = dt * jnp.exp(cum_last - cum)
    chunk_decay = jnp.exp(cum_last)

    nrow = lax.broadcasted_iota(jnp.int32, (n, n), 0)
    ncol = lax.broadcasted_iota(jnp.int32, (n, n), 1)
    eye_n = (nrow == ncol).astype(BF16)
    lane_c = lax.broadcasted_iota(jnp.int32, (c, pair_w), 1) < SSM_HEAD_DIM
    lane_n = lax.broadcasted_iota(jnp.int32, (n, pair_w), 1) < SSM_HEAD_DIM
    lane_1 = lax.broadcasted_iota(jnp.int32, (1, pair_w), 1) < SSM_HEAD_DIM

    heads_per_group = SSM_HEADS // SSM_GROUPS
    for g in range(SSM_GROUPS):
        b_lo = SSM_INNER + g * n
        c_lo = SSM_INNER + SSM_GROUPS * n + g * n
        bm = act[:, b_lo:b_lo + n].astype(BF16)
        cm = act[:, c_lo:c_lo + n]
        cb = lax.dot_general(cm.astype(BF16), bm, NT_DIMS, preferred_element_type=F32)
        bm_t = lax.dot_general(eye_n, bm, NT_DIMS, preferred_element_type=F32).astype(BF16)
        for pair in range(heads_per_group // 2):
            ha = g * heads_per_group + 2 * pair
            hb = ha + 1
            lo = ha * SSM_HEAD_DIM
            x_pair = act[:, lo:lo + pair_w]
            st = ht_ref[0, :, lo:lo + pair_w]
            dt_pair = jnp.where(lane_c, dt[:, ha:ha + 1], dt[:, hb:hb + 1])
            dte_pair = jnp.where(lane_c, dt_end[:, ha:ha + 1], dt_end[:, hb:hb + 1])
            rhs = jnp.concatenate([(x_pair * dt_pair).astype(BF16), st.astype(BF16)], axis=0)
            ys = []
            for h in (ha, hb):
                seg = cum[:, h:h + 1] - cum_t[h:h + 1, :]
                decay = jnp.exp(jnp.where(tril, seg, -jnp.inf))
                lhs = jnp.concatenate(
                    [(cb * decay).astype(BF16), (cm * exp_cum[:, h:h + 1]).astype(BF16)], axis=1)
                ys.append(jnp.dot(lhs, rhs, preferred_element_type=F32))
            y_ref[:, lo:lo + pair_w] = jnp.where(lane_c, ys[0], ys[1])
            upd = jnp.dot(bm_t, (x_pair * dte_pair).astype(BF16), preferred_element_type=F32)
            cd_pair = jnp.where(lane_1, chunk_decay[:, ha:ha + 1], chunk_decay[:, hb:hb + 1])
            ht_ref[0, :, lo:lo + pair_w] = st * cd_pair + upd

    y = y_ref[...] + dskip_ref[...] * act[:, :SSM_INNER]
    zz = z_ref[0]
    gated = y * (zz * _sigmoid(zz))
    yn_ref[0] = ((gated * _rms_scale(gated)) * norm_ref[...]).astype(yn_ref.dtype)


def _ssd(xbc, dt, z, ctx8, h0t, conv_w, conv_b, dt_bias, a_log, d_skip, ssm_norm, c):
    b, l, _ = xbc.shape
    assert l % c == 0 and c % SUBLANES_V7X == 0
    n = SSM_STATE
    per_b = lambda shape: pl.BlockSpec(shape, lambda i, j: (i, 0, 0))
    chunk = lambda w: pl.BlockSpec((1, c, w), lambda i, j: (i, j, 0))
    full = lambda shape: pl.BlockSpec(shape, lambda i, j: (0, 0))
    block_bytes = (_nbytes((c, CONV_DIM), F32) + _nbytes((c, LANES_V7X), F32) + _nbytes((c, SSM_INNER), F32)
                   + _nbytes((8, CONV_DIM), F32) + 2 * _nbytes((n, SSM_INNER), F32)
                   + _nbytes((c, SSM_INNER), BF16) + 8 * _nbytes((8, CONV_DIM), F32))
    scratch_bytes = _nbytes((c + 8, CONV_DIM), F32) + _nbytes((c, SSM_INNER), F32)
    return pl.pallas_call(
        functools.partial(_ssd_kernel, c=c),
        grid=(b, l // c),
        in_specs=[chunk(CONV_DIM), chunk(SSM_HEADS), chunk(SSM_INNER), per_b((1, 8, CONV_DIM)),
                  per_b((1, n, SSM_INNER)), full((CONV_W, CONV_DIM)), full((1, CONV_DIM)),
                  full((1, SSM_HEADS)), full((1, SSM_HEADS)), full((1, SSM_INNER)), full((1, SSM_INNER))],
        out_specs=[chunk(SSM_INNER), per_b((1, n, SSM_INNER))],
        out_shape=[jax.ShapeDtypeStruct((b, l, SSM_INNER), BF16),
                   jax.ShapeDtypeStruct((b, n, SSM_INNER), F32)],
        scratch_shapes=[pltpu.VMEM((c + 8, CONV_DIM), F32), pltpu.VMEM((c, SSM_INNER), F32)],
        compiler_params=pltpu.CompilerParams(
            dimension_semantics=("parallel", "arbitrary"),
            vmem_limit_bytes=_vmem_limit(block_bytes, scratch_bytes)),
        name="ssd",
    )(xbc, dt, z, ctx8, h0t, conv_w, conv_b.reshape(1, -1), dt_bias.reshape(1, -1), a_log.reshape(1, -1),
      jnp.repeat(d_skip, SSM_HEAD_DIM).reshape(1, -1), ssm_norm.reshape(1, -1))


def _sb_attn_kernel(*refs, tq, tkp, n_past, hps, scale):
    if n_past:
        q_ref, kn_ref, vn_ref, kp_ref, vp_ref, o_ref, acc_ref, run_ref = refs
    else:
        q_ref, kn_ref, vn_ref, o_ref, acc_ref, run_ref, w_ref = refs
    qi = pl.program_id(2)
    width = hps * SB_HEAD_DIM
    m = hps * tq
    q = (q_ref[0].astype(F32) * scale).astype(BF16)
    head_of_lane = lax.broadcasted_iota(jnp.int32, (tq, width), 1) // SB_HEAD_DIM
    zero = jnp.zeros_like(q)
    q_stack = jnp.concatenate([jnp.where(head_of_lane == h, q, zero) for h in range(hps)], axis=0)

    def suffix_ones(t):
        r = lax.broadcasted_iota(jnp.int32, (t, t), 0)
        s = lax.broadcasted_iota(jnp.int32, (t, t), 1)
        return jnp.where(r > s, -1.0, 0.0).astype(BF16)

    acc_ref[...] = jnp.zeros_like(acc_ref)
    run_ref[...] = jnp.zeros_like(run_ref)
    sign_bit = jnp.int32(-2 ** 31)

    def apply(w, v):
        acc_ref[...] += jnp.dot(w, v, preferred_element_type=F32)

    def weights(k, mask, due=None):
        ck = k.shape[0]
        sub = min(ck, SB_SUB)
        ones = suffix_ones(sub)
        zs = lax.dot_general(q_stack, k, NT_DIMS, preferred_element_type=F32)
        neg_abs = lax.bitcast_convert_type(lax.bitcast_convert_type(zs, jnp.int32) | sign_bit, F32)
        drop = jnp.maximum(zs, 0.0) + jnp.log(1.0 + jnp.exp(neg_abs))
        if mask is not None:
            drop = jnp.where(mask, drop, 0.0)
        drop_b = drop.astype(BF16)
        log_beta = zs - drop
        if due is not None:
            apply(*due)
        run = run_ref[...]
        ws = []
        for j in reversed(range(ck // sub)):
            sl = slice(j * sub, (j + 1) * sub)
            suffix = jnp.dot(drop_b[:, sl], ones, preferred_element_type=F32)
            w = jnp.exp(log_beta[:, sl] + suffix + run)
            if mask is not None:
                w = jnp.where(mask[:, sl], w, 0.0)
            ws.insert(0, w.astype(BF16))
            first = drop_b[:, j * sub:j * sub + 1].astype(F32)
            run = run + (suffix[:, 0:1] - first)
        run_ref[...] = run
        return ws[0] if len(ws) == 1 else jnp.concatenate(ws, axis=1)

    def new_chunk(ref, c):
        return ref[0, pl.ds(pl.multiple_of(c * tq, tq), tq), :]

    r = lax.rem(lax.broadcasted_iota(jnp.int32, (m, tq), 0), tq)
    s = lax.broadcasted_iota(jnp.int32, (m, tq), 1)
    w_diag = weights(new_chunk(kn_ref, qi), s < r)

    if n_past:
        apply(w_diag, new_chunk(vn_ref, qi))

        def new_body(i, carry):
            apply(weights(new_chunk(kn_ref, qi - 1 - i), None), new_chunk(vn_ref, qi - 1 - i))
            return carry

        lax.fori_loop(0, qi, new_body, 0)

        def past_body(i, carry):
            off = pl.multiple_of((n_past - 1 - i) * tkp, tkp)
            apply(weights(kp_ref[0, pl.ds(off, tkp), :].astype(BF16), None),
                  vp_ref[0, pl.ds(off, tkp), :].astype(BF16))
            return carry

        lax.fori_loop(0, n_past, past_body, 0)
    else:
        w_ref[...] = w_diag

        def new_body(i, carry):
            due = (w_ref[...], new_chunk(vn_ref, qi - i))
            w_ref[...] = weights(new_chunk(kn_ref, qi - 1 - i), None, due)
            return carry

        lax.fori_loop(0, qi, new_body, 0)
        apply(w_ref[...], new_chunk(vn_ref, 0))

    out = acc_ref[0:tq, :]
    for h in range(1, hps):
        out = jnp.where(head_of_lane == h, acc_ref[h * tq:(h + 1) * tq, :], out)
    o_ref[0] = out.astype(o_ref.dtype)


def _sb_attn(q, k_new, v_new, k_past, v_past, tq, tkp, hps):
    b, l, _ = q.shape
    assert l % tq == 0 and SB_HEADS % hps == 0
    width = hps * SB_HEAD_DIM
    m = hps * tq
    n_past = 0
    tile = pl.BlockSpec((1, tq, width), lambda i, hq, qi: (i, qi, hq))
    whole = lambda length: pl.BlockSpec((1, length, width), lambda i, hq, qi: (i, 0, hq))
    in_specs = [tile, whole(l), whole(l)]
    args = [q, k_new, v_new]
    block_bytes = 2 * _nbytes((tq, width), BF16) + 2 * _nbytes((l, width), BF16)
    if k_past is not None:
        p = k_past.shape[1]
        assert p % tkp == 0
        n_past = p // tkp
        in_specs += [whole(p), whole(p)]
        args += [k_past, v_past]
        block_bytes += 2 * _nbytes((p, width), F32)
    scratch = [pltpu.VMEM((m, width), F32), pltpu.VMEM((m, 1), F32)]
    scratch_bytes = _nbytes((m, width), F32) + _nbytes((m, LANES_V7X), F32)
    if not n_past:
        scratch.append(pltpu.VMEM((m, tq), BF16))
        scratch_bytes += _nbytes((m, tq), BF16)
    return pl.pallas_call(
        functools.partial(_sb_attn_kernel, tq=tq, tkp=tkp, n_past=n_past, hps=hps, scale=SB_HEAD_DIM ** -0.5),
        grid=(b, SB_HEADS // hps, l // tq),
        in_specs=in_specs,
        out_specs=tile,
        out_shape=jax.ShapeDtypeStruct((b, l, SB_INNER), BF16),
        scratch_shapes=scratch,
        compiler_params=pltpu.CompilerParams(
            dimension_semantics=("parallel", "parallel", "arbitrary"),
            vmem_limit_bytes=_vmem_limit(block_bytes, scratch_bytes)),
        name="sb_attn",
    )(*args)


def _merge_mem_kernel(x_ref, yn_ref, ysb_ref, gs_ref, gb_ref, mk_ref, mv_ref, wssm_ref, wsb_ref, wout_ref,
                      nq_ref, wq_ref, wo_ref, o_ref, *, scale):
    y_ssm = jnp.dot(yn_ref[0], wssm_ref[...], preferred_element_type=F32)
    y_sb = jnp.dot(ysb_ref[0], wsb_ref[...], preferred_element_type=F32)
    merged = _sigmoid(gs_ref[0]) * y_ssm + _sigmoid(gb_ref[0]) * y_sb
    x1 = x_ref[0] + jnp.dot(merged.astype(BF16), wout_ref[...], preferred_element_type=F32)
    hq = ((x1 * _rms_scale(x1)) * nq_ref[...]).astype(BF16)
    q = jnp.dot(hq, wq_ref[...], preferred_element_type=F32)
    outs = []
    for h in range(MEM_HEADS):
        sl = slice(h * MEM_HEAD_DIM, (h + 1) * MEM_HEAD_DIM)
        mk = mk_ref[0, :, sl].astype(BF16)
        mv = mv_ref[0, :, sl].astype(BF16)
        s = lax.dot_general(q[:, sl].astype(BF16), mk, NT_DIMS, preferred_element_type=F32) * scale
        e = jnp.exp(s - jnp.max(s, axis=-1, keepdims=True))
        p = e / jnp.sum(e, axis=-1, keepdims=True)
        outs.append(jnp.dot(p.astype(BF16), mv, preferred_element_type=F32))
    o = jnp.concatenate(outs, axis=1)
    o_ref[0] = x1 + jnp.dot(o.astype(BF16), wo_ref[...], preferred_element_type=F32)


def _merge_mem(x, yn, ysb, gs, gb, mk, mv, w_ssm, w_sb, w_out, norm_q, w_q, w_o, tm):
    b, l, d = x.shape
    m = mk.shape[1]
    assert l % tm == 0
    tok = lambda w: pl.BlockSpec((1, tm, w), lambda i, j: (i, j, 0))
    per_b = pl.BlockSpec((1, m, d), lambda i, j: (i, 0, 0))
    full = lambda a: pl.BlockSpec(a.shape, lambda i, j: (0, 0))
    nq = norm_q.reshape(1, d)
    weights = (w_ssm, w_sb, w_out, nq, w_q, w_o)
    block_bytes = (4 * _nbytes((tm, d), F32) + _nbytes((tm, SSM_INNER + SB_INNER), BF16)
                   + 2 * _nbytes((m, d), F32) + sum(_nbytes(w.shape, w.dtype) for w in weights))
    return pl.pallas_call(
        functools.partial(_merge_mem_kernel, scale=MEM_HEAD_DIM ** -0.5),
        grid=(b, l // tm),
        in_specs=[tok(d), tok(SSM_INNER), tok(SB_INNER), tok(d), tok(d), per_b, per_b] + [full(w) for w in weights],
        out_specs=tok(d),
        out_shape=jax.ShapeDtypeStruct((b, l, d), F32),
        compiler_params=pltpu.CompilerParams(
            dimension_semantics=("parallel", "parallel"), vmem_limit_bytes=_vmem_limit(block_bytes)),
        name="merge_mem",
    )(x, yn, ysb, gs, gb, mk, mv, *weights)


def _route(logits):
    lane = lax.broadcasted_iota(jnp.int32, logits.shape, 1)
    big = jnp.int32(LANES_V7X)
    neg = -jnp.inf
    is_group = lane < MOE_GROUPS
    gl = jnp.where(is_group, logits, neg)
    ge = jnp.where(is_group, jnp.exp(gl - jnp.max(gl, axis=-1, keepdims=True)), 0.0)
    pg = ge / jnp.sum(ge, axis=-1, keepdims=True)
    p_sel = jnp.max(pg, axis=-1, keepdims=True)
    g_sel = jnp.min(jnp.where(is_group & (pg == p_sel), lane, big), axis=-1, keepdims=True)
    e_idx = lane - MOE_GROUPS
    in_group = (e_idx >= 0) & (e_idx < N_EXPERTS) & ((e_idx // EXPERTS_PER_GROUP) == g_sel)
    le = jnp.where(in_group, logits, neg)
    v1 = jnp.max(le, axis=-1, keepdims=True)
    i1 = jnp.min(jnp.where(in_group & (le == v1), lane, big), axis=-1, keepdims=True)
    rest = in_group & (lane != i1)
    le2 = jnp.where(rest, logits, neg)
    v2 = jnp.max(le2, axis=-1, keepdims=True)
    i2 = jnp.min(jnp.where(rest & (le2 == v2), lane, big), axis=-1, keepdims=True)
    e2 = jnp.exp(v2 - v1)
    den = 1.0 + e2
    dense_w = (jnp.where(lane == i1, (1.0 / den) * p_sel, 0.0)
               + jnp.where(lane == i2, (e2 / den) * p_sel, 0.0))
    return dense_w, g_sel


def _moe_hidden(x_ref, nf_ref):
    x = x_ref[...]
    return ((x * _rms_scale(x)) * nf_ref[...]).astype(BF16)


def _moe_rank_kernel(x_ref, nf_ref, wr_ref, info_ref, cnt_ref, carry_ref):
    i = pl.program_id(0)

    @pl.when(i == 0)
    def _():
        carry_ref[...] = jnp.zeros_like(carry_ref)

    hb = _moe_hidden(x_ref, nf_ref)
    _, g_sel = _route(jnp.dot(hb, wr_ref[...], preferred_element_type=F32))
    tm = hb.shape[0]
    lane = lax.broadcasted_iota(jnp.int32, (tm, LANES_V7X), 1)
    onehot = lane == g_sel
    r = lax.broadcasted_iota(jnp.int32, (tm, tm), 0)
    c = lax.broadcasted_iota(jnp.int32, (tm, tm), 1)
    before = jnp.dot((c < r).astype(BF16), onehot.astype(BF16), preferred_element_type=F32) + carry_ref[...]
    rank = jnp.sum(jnp.where(onehot, before, 0.0), axis=-1, keepdims=True)
    info = jnp.where(lane == 0, g_sel.astype(F32), jnp.where(lane == 1, rank, 0.0))
    info_ref[...] = info.astype(jnp.int32)
    carry_ref[...] += jnp.sum(onehot.astype(F32), axis=0, keepdims=True)
    cnt_ref[...] = carry_ref[...].astype(jnp.int32)


def _moe_experts_kernel(tile_ref, exp_ref, first_ref, last_ref, act_ref, x_ref, nf_ref, wr_ref, wg_ref, wu_ref,
                        wd_ref, nfin_ref, y_ref, h_ref, dw_ref, acc_ref, *, final_norm):
    w = pl.program_id(0)

    @pl.when(first_ref[w] == 1)
    def _():
        hb = _moe_hidden(x_ref, nf_ref)
        h_ref[...] = hb
        dw_ref[...] = _route(jnp.dot(hb, wr_ref[...], preferred_element_type=F32))[0]
        acc_ref[...] = jnp.zeros_like(acc_ref)

    @pl.when(act_ref[w] == 1)
    def _():
        hb = h_ref[...]
        gate = jnp.dot(hb, wg_ref[0], preferred_element_type=F32)
        up = jnp.dot(hb, wu_ref[0], preferred_element_type=F32)
        he = (gate * _sigmoid(gate)) * up
        down = jnp.dot(he.astype(BF16), wd_ref[0], preferred_element_type=F32)
        lane = lax.broadcasted_iota(jnp.int32, dw_ref.shape, 1)
        dcol = jnp.sum(jnp.where(lane == exp_ref[w] + MOE_GROUPS, dw_ref[...], 0.0), axis=-1, keepdims=True)
        acc_ref[...] += dcol * down

    @pl.when(last_ref[w] == 1)
    def _():
        xo = x_ref[...] + acc_ref[...]
        y_ref[...] = (xo * _rms_scale(xo)) * nfin_ref[...] if final_norm else xo


def _permute_rows_kernel(idx_ref, src_ref, dst_ref, sem, *, rows, scatter):
    def row_copy(r, k):
        src_row, dst_row = (r, k) if scatter else (k, r)
        return pltpu.make_async_copy(src_ref.at[pl.ds(src_row, 1)], dst_ref.at[pl.ds(dst_row, 1)], sem.at[0])

    def start(i, carry):
        for prio in range(DMA_PRIORITIES_V7X):
            r = DMA_PRIORITIES_V7X * i + prio
            row_copy(r, idx_ref[0, 0, r]).start(priority=prio)
        return carry

    def wait(r, carry):
        row_copy(0, 0).wait()
        return carry

    lax.fori_loop(0, rows // DMA_PRIORITIES_V7X, start, 0, unroll=4)
    lax.fori_loop(0, rows, wait, 0, unroll=8)


def _permute_rows(src, idx, scatter, name):
    t, d = src.shape
    rows = _tile(t, PERMUTE_ROWS)
    tile = pl.BlockSpec((rows, d), lambda s: (s, 0))
    hbm = pl.BlockSpec(memory_space=pl.ANY)
    return pl.pallas_call(
        functools.partial(_permute_rows_kernel, rows=rows, scatter=scatter),
        grid=(t // rows,),
        in_specs=[pl.BlockSpec((1, 1, rows), lambda s: (s, 0, 0), memory_space=pltpu.SMEM),
                  tile if scatter else hbm],
        out_specs=hbm if scatter else tile,
        out_shape=jax.ShapeDtypeStruct((t, d), src.dtype),
        scratch_shapes=[pltpu.SemaphoreType.DMA((1,))],
        compiler_params=pltpu.CompilerParams(
            dimension_semantics=("arbitrary",), vmem_limit_bytes=_vmem_limit(_nbytes((rows, d), src.dtype))),
        name=name,
    )(idx.reshape(t // rows, 1, rows), src)


def _moe_plan(gid, rank, counts, n_tiles, tm):
    i32 = jnp.int32
    ends = jnp.cumsum(counts).astype(i32)
    pos = (ends - counts)[gid] + rank
    first_row = jnp.arange(n_tiles, dtype=i32) * tm
    groups_before = lambda row: jnp.sum((ends[None, :] <= row[:, None]).astype(i32), axis=1)
    g_lo, g_hi = groups_before(first_row), groups_before(first_row + (tm - 1))
    n_items = EXPERTS_PER_GROUP * (g_hi - g_lo + 1)
    item_end = jnp.cumsum(n_items).astype(i32)
    n_work = EXPERTS_PER_GROUP * (n_tiles + MOE_GROUPS - 1)
    w = jnp.arange(n_work, dtype=i32)
    tile = jnp.minimum(jnp.sum((item_end[None, :] <= w[:, None]).astype(i32), axis=1), n_tiles - 1)
    local = w - (item_end - n_items)[tile]
    active = w < item_end[-1]
    expert = jnp.where(active, EXPERTS_PER_GROUP * g_lo[tile] + local,
                       EXPERTS_PER_GROUP * g_hi[-1] + EXPERTS_PER_GROUP - 1)
    first = active & (local == 0)
    last = active & (local == n_items[tile] - 1)
    return pos.astype(i32), (tile, expert.astype(i32), first.astype(i32), last.astype(i32), active.astype(i32))


def _moe(x2d, norm_ffn, w_router, w_gate, w_up, w_down, norm_final, final_norm, tm):
    t, d = x2d.shape
    assert t % tm == 0 and t < 2 ** 24
    n_tiles = t // tm
    n_e, _, ff = w_gate.shape
    nf, nfin = norm_ffn.reshape(1, d), norm_final.reshape(1, d)
    const = lambda a: pl.BlockSpec(a.shape, lambda i, *_: (0, 0))

    rank_bytes = _nbytes((tm, d), F32) + _nbytes(w_router.shape, BF16) + _nbytes((tm, LANES_V7X), F32)
    info, cnt = pl.pallas_call(
        _moe_rank_kernel,
        grid=(n_tiles,),
        in_specs=[pl.BlockSpec((tm, d), lambda i: (i, 0)), const(nf), const(w_router)],
        out_specs=[pl.BlockSpec((tm, LANES_V7X), lambda i: (i, 0)), pl.BlockSpec((1, LANES_V7X), lambda i: (0, 0))],
        out_shape=[jax.ShapeDtypeStruct((t, LANES_V7X), jnp.int32), jax.ShapeDtypeStruct((1, LANES_V7X), jnp.int32)],
        scratch_shapes=[pltpu.VMEM((1, LANES_V7X), F32)],
        compiler_params=pltpu.CompilerParams(
            dimension_semantics=("arbitrary",), vmem_limit_bytes=_vmem_limit(rank_bytes)),
        name="moe_rank",
    )(x2d, nf, w_router)
    pos, plan = _moe_plan(info[:, 0], info[:, 1], cnt[0, :MOE_GROUPS], n_tiles, tm)

    xs = _permute_rows(x2d, pos, True, "moe_sort")
    by_tile = lambda wi, tile, *_: (tile[wi], 0)
    by_expert = lambda wi, tile, exp, *_: (exp[wi], 0, 0)
    block_bytes = (2 * _nbytes((tm, d), F32) + 2 * _nbytes((1, d), F32) + _nbytes(w_router.shape, BF16)
                   + 3 * _nbytes((d, ff), BF16))
    scratch_bytes = _nbytes((tm, d), BF16) + _nbytes((tm, LANES_V7X), F32) + _nbytes((tm, d), F32)
    ys = pl.pallas_call(
        functools.partial(_moe_experts_kernel, final_norm=final_norm),
        grid_spec=pltpu.PrefetchScalarGridSpec(
            num_scalar_prefetch=len(plan),
            grid=(plan[0].shape[0],),
            in_specs=[pl.BlockSpec((tm, d), by_tile), const(nf), const(w_router),
                      pl.BlockSpec((1, d, ff), by_expert), pl.BlockSpec((1, d, ff), by_expert),
                      pl.BlockSpec((1, ff, d), by_expert), const(nfin)],
            out_specs=pl.BlockSpec((tm, d), by_tile),
            scratch_shapes=[pltpu.VMEM((tm, d), BF16), pltpu.VMEM((tm, LANES_V7X), F32), pltpu.VMEM((tm, d), F32)]),
        out_shape=jax.ShapeDtypeStruct((t, d), F32),
        compiler_params=pltpu.CompilerParams(
            dimension_semantics=("arbitrary",), vmem_limit_bytes=_vmem_limit(block_bytes, scratch_bytes)),
        name="moe_experts",
    )(*plan, xs, nf, w_router, w_gate, w_up, w_down, nfin)
    return _permute_rows(ys, pos, False, "moe_unsort")


def _tile(n, pref):
    t = min(n, pref)
    while n % t:
        t //= 2
    return t


def _split_w_in(w_in):
    offs = [0]
    for s in IN_SIZES:
        offs.append(offs[-1] + s)
    return [w_in[:, offs[i]:offs[i + 1]].astype(BF16) for i in range(len(IN_SIZES))]


def _group_step(x, conv_ctx, ssm_h0, past_k, past_v, mem_k, mem_v, lw):
    b, l, d = x.shape
    t = b * l
    x2d = x.reshape(t, d)
    w_z, w_xbc, w_dt, w_q, w_k, w_v, w_gs, w_gb = lw["w_in_parts"]
    flat32 = ((F32, None),)
    kv_out = ((F32, SB_HEADS), (BF16, None))
    z, xbc, dt, q, k, kb, v, vb, gs, gb = _norm_proj(
        x2d, lw["norm_mix"], [w_z, w_xbc, w_dt, w_q, w_k, w_v, w_gs, w_gb],
        [flat32, flat32, flat32, ((BF16, None),), kv_out, kv_out, flat32, flat32],
        _tile(t, 256), "in_proj")
    r3 = lambda a: a.reshape(b, l, a.shape[-1])

    h0t = ssm_h0.transpose(0, 3, 1, 2).reshape(b, SSM_STATE, SSM_INNER)
    ctx8 = jnp.pad(conv_ctx, ((0, 0), (SUBLANES_V7X - (CONV_W - 1), 0), (0, 0)))
    yn, ht = _ssd(r3(xbc), r3(dt), r3(z), ctx8, h0t, lw["conv_w"], lw["conv_b"], lw["dt_bias"], lw["a_log"],
                  lw["d_skip"], lw["ssm_norm"], _tile(l, 128))
    ssm_new = ht.reshape(b, SSM_STATE, SSM_HEADS, SSM_HEAD_DIM).transpose(0, 2, 3, 1)
    assert l >= CONV_W - 1
    conv_new = r3(xbc)[:, l - (CONV_W - 1):, :]

    tq = _tile(l, SB_CHUNK)
    hps = min(SB_HEADS, SB_STACK_ROWS // tq)
    ysb = _sb_attn(r3(q), r3(kb), r3(vb), past_k, past_v, tq, SB_CHUNK, hps)

    x_mem = _merge_mem(x, yn, ysb, r3(gs), r3(gb), mem_k, mem_v, lw["w_ssm_br"], lw["w_sb_br"], lw["w_out"],
                       lw["norm_mem_q"], lw["w_mem_q"], lw["w_mem_o"], _tile(l, 512))
    kh = k.reshape(b, l, SB_HEADS, SB_HEAD_DIM)
    vh = v.reshape(b, l, SB_HEADS, SB_HEAD_DIM)
    return x_mem, conv_new, ssm_new, kh, vh


def kernel(x_prompt, x_sample, cache_sb_k, cache_sb_v, state_ssm, state_conv, cache_mem_k, cache_mem_v, mem_prompt, norm_mix, w_in, conv_w, conv_b, dt_bias, a_log, d_skip, ssm_norm, w_ssm_br, w_sb_br, w_out, norm_mem_q, norm_mem_kv, w_mem_q, w_mem_kv, w_mem_o, norm_ffn, w_group_router, w_expert_router, w_gate_e, w_up_e, w_down_e, norm_final):
    depth = norm_mix.shape[0]
    bp, lp, d = x_prompt.shape
    bs, ls, _ = x_sample.shape
    xp, xs = x_prompt, x_sample
    outs = {name: [] for name in ("skp", "svp", "ssp", "scp", "mkp", "mvp", "sks", "svs", "sss", "scs")}
    for l in range(depth):
        router = jnp.concatenate([w_group_router[l], w_expert_router[l]], axis=1)
        router = jnp.pad(router, ((0, 0), (0, LANES_V7X - router.shape[1]))).astype(BF16)
        lw = dict(
            norm_mix=norm_mix[l], w_in_parts=_split_w_in(w_in[l]), conv_w=conv_w[l], conv_b=conv_b[l],
            dt_bias=dt_bias[l], a_log=a_log[l], d_skip=d_skip[l], ssm_norm=ssm_norm[l],
            w_ssm_br=w_ssm_br[l].astype(BF16), w_sb_br=w_sb_br[l].astype(BF16), w_out=w_out[l].astype(BF16),
            norm_mem_q=norm_mem_q[l], w_mem_q=w_mem_q[l].astype(BF16), w_mem_o=w_mem_o[l].astype(BF16))
        moe_w = (norm_ffn[l], router, w_gate_e[l].astype(BF16), w_up_e[l].astype(BF16), w_down_e[l].astype(BF16))
        last = l == depth - 1

        m = mem_prompt.shape[1]
        w_kv = w_mem_kv[l].astype(BF16)
        mk_p, mv_p = _norm_proj(mem_prompt.reshape(bp * m, d), norm_mem_kv[l], [w_kv[:, :d], w_kv[:, d:]],
                                [((F32, None),), ((F32, None),)], _tile(bp * m, 256), "mem_kv")
        mk_p, mv_p = mk_p.reshape(bp, m, d), mv_p.reshape(bp, m, d)
        conv0 = jnp.zeros((bp, CONV_W - 1, CONV_DIM), xp.dtype)
        h0 = jnp.zeros((bp, SSM_HEADS, SSM_HEAD_DIM, SSM_STATE), xp.dtype)
        xp, conv_p, ssm_p, k_p, v_p = _group_step(xp, conv0, h0, None, None, mk_p, mv_p, lw)
        xp = _moe(xp.reshape(bp * lp, d), *moe_w, norm_final, last, _tile(bp * lp, MOE_TILE)).reshape(bp, lp, d)
        outs["skp"].append(k_p); outs["svp"].append(v_p); outs["ssp"].append(ssm_p); outs["scp"].append(conv_p)
        outs["mkp"].append(mk_p.reshape(bp, m, MEM_HEADS, MEM_HEAD_DIM))
        outs["mvp"].append(mv_p.reshape(bp, m, MEM_HEADS, MEM_HEAD_DIM))

        past = cache_sb_k.shape[2]
        ms = cache_mem_k.shape[2]
        xs, conv_s, ssm_s, k_s, v_s = _group_step(
            xs, state_conv[l], state_ssm[l], cache_sb_k[l].reshape(bs, past, SB_INNER),
            cache_sb_v[l].reshape(bs, past, SB_INNER), cache_mem_k[l].reshape(bs, ms, d),
            cache_mem_v[l].reshape(bs, ms, d), lw)
        xs = _moe(xs.reshape(bs * ls, d), *moe_w, norm_final, last, _tile(bs * ls, MOE_TILE)).reshape(bs, ls, d)
        outs["sks"].append(k_s); outs["svs"].append(v_s); outs["sss"].append(ssm_s); outs["scs"].append(conv_s)

    st = lambda name: jnp.stack(outs[name])
    return (xp, xs, st("skp"), st("svp"), st("ssp"), st("scp"), st("mkp"), st("mvp"),
            st("sks"), st("svs"), st("sss"), st("scs"))
```
